```python
import math
import jax, jax.numpy as jnp
from jax import lax
import numpy as np

D_MODEL = 4096
BATCH = 4
SEQ = 2048
DEPTH = 2
DEC_BATCH = 8
DEC_SEQ = 8
PAST_LEN = 16384
PAGE_SIZE = 128

HEAD_DIM = 128
H_A = D_MODEL // (2 * HEAD_DIM)
H_B = D_MODEL // (2 * HEAD_DIM)
H_AB = H_A + H_B
IDX_HEADS = 64
IDX_DIM = 128
IDX_SCALE = (IDX_HEADS * IDX_DIM) ** -0.5
TOPK_MAX = 256
A_QBLOCK = 64
B_QBLOCK = 128
FORGET_BIAS = 3.0
DK_C = 256
H_C = D_MODEL // DK_C
DV_C = 2 * DK_C
RET_CHUNK = 128
ROPE_BASE = 10000.0
D_FF = 4 * D_MODEL
N_ATT_LAYERS = (DEPTH + 1) // 2
N_RET_LAYERS = DEPTH // 2
ALPHA = (2 * DEPTH) ** 0.25
BETA = (8 * DEPTH) ** -0.25
LN_EPS = 1e-5
SPLIT_ATT = (H_A * HEAD_DIM, H_A * HEAD_DIM, H_A * HEAD_DIM, IDX_HEADS * IDX_DIM, IDX_DIM, IDX_HEADS,
             H_B * HEAD_DIM, H_B * HEAD_DIM, H_B * HEAD_DIM, H_B)
SPLIT_RET = (H_C * DK_C, H_C * DK_C, H_C * DV_C, H_C * DV_C)
D_IN_ATT = sum(SPLIT_ATT)
D_IN_RET = sum(SPLIT_RET)

kernel_name = 'dsa_fox_retention_hybrid_step'

F32 = jnp.float32


def split_cols(z, sizes):
    return jnp.split(z, np.cumsum(sizes)[:-1].tolist(), axis=-1)


def layer_norm(x, g, b):
    xf = x.astype(F32)
    mu = jnp.mean(xf, axis=-1, keepdims=True)
    var = jnp.mean(jnp.square(xf - mu), axis=-1, keepdims=True)
    return ((xf - mu) * lax.rsqrt(var + LN_EPS) * g + b).astype(x.dtype)


def sq_relu_mlp(x, w_up, w_down):
    return jnp.square(jax.nn.relu(x @ w_up)) @ w_down


def att_projections(x, w_in, b_f):
    B, T, _ = x.shape
    qa, ka, va, qi, ki, wi, qb, kb, vb, fb = split_cols(x @ w_in, SPLIT_ATT)
    hd = lambda z, h: z.reshape(B, T, h, -1)
    k_all = jnp.concatenate([hd(ka, H_A), hd(kb, H_B)], axis=2)
    v_all = jnp.concatenate([hd(va, H_A), hd(vb, H_B)], axis=2)
    logf = jax.nn.log_sigmoid((fb + b_f).astype(F32))
    return hd(qa, H_A), hd(qi, IDX_HEADS), wi, ki, hd(qb, H_B), k_all, v_all, logf


def indexer_scores(qi, wi, ki):
    s = jax.nn.relu(jnp.einsum('bthd,bsd->bths', qi, ki, preferred_element_type=F32))
    return jnp.einsum('bths,bth->bts', s, wi.astype(F32) * IDX_SCALE)


def sparse_attend(q, k_sel, v_sel, valid):
    s = jnp.einsum('bthd,btkhd->bthk', q, k_sel, preferred_element_type=F32) * (HEAD_DIM ** -0.5)
    p = jax.nn.softmax(jnp.where(valid[:, :, None, :], s, -jnp.inf), axis=-1)
    return jnp.einsum('bthk,btkhd->bthd', p.astype(v_sel.dtype), v_sel)


def dsa_prompt(qa, ka, va, qi, wi, ki, topk):
    B, S = qa.shape[:2]
    kpos = jnp.arange(S)
    bidx = jnp.arange(B)[:, None, None]

    def blk(i):
        t0 = i * A_QBLOCK
        qpos = t0 + jnp.arange(A_QBLOCK)
        sl = lambda z: lax.dynamic_slice_in_dim(z, t0, A_QBLOCK, axis=1)
        sc = indexer_scores(sl(qi), sl(wi), ki)
        sc = jnp.where((kpos[None, :] <= qpos[:, None])[None], sc, -jnp.inf)
        _, idx = lax.top_k(sc, topk)
        valid = idx <= qpos[None, :, None]
        return sparse_attend(sl(qa), ka[bidx, idx], va[bidx, idx], valid)

    out = lax.map(blk, jnp.arange(S // A_QBLOCK))
    return jnp.moveaxis(out, 0, 1).reshape(B, S, H_A, HEAD_DIM)


def fox_prompt(qb, kb, vb, logf):
    B, S = qb.shape[:2]
    c = jnp.moveaxis(jnp.cumsum(logf, axis=1), 1, 2)
    kpos = jnp.arange(S)

    def blk(i):
        t0 = i * B_QBLOCK
        qpos = t0 + jnp.arange(B_QBLOCK)
        q = lax.dynamic_slice_in_dim(qb, t0, B_QBLOCK, axis=1)
        ct = lax.dynamic_slice_in_dim(c, t0, B_QBLOCK, axis=2)
        s = jnp.einsum('bthd,bshd->bhts', q, kb, preferred_element_type=F32) * (HEAD_DIM ** -0.5)
        s = s + ct[..., None] - c[:, :, None, :]
        s = jnp.where(kpos[None, :] <= qpos[:, None], s, -jnp.inf)
        p = jax.nn.softmax(s, axis=-1)
        return jnp.einsum('bhts,bshd->bthd', p.astype(vb.dtype), vb)

    out = lax.map(blk, jnp.arange(S // B_QBLOCK))
    return jnp.moveaxis(out, 0, 1).reshape(B, S, H_B, HEAD_DIM)


def att_mixer_prompt(x, w_in, b_f, w_out):
    B, S, _ = x.shape
    qa, qi, wi, ki, qb, k_all, v_all, logf = att_projections(x, w_in, b_f)
    topk = min(TOPK_MAX, S // 4)
    oa = dsa_prompt(qa, k_all[:, :, :H_A], v_all[:, :, :H_A], qi, wi, ki, topk)
    ob = fox_prompt(qb, k_all[:, :, H_A:], v_all[:, :, H_A:], logf)
    y = jnp.concatenate([oa, ob], axis=2).reshape(B, S, H_AB * HEAD_DIM) @ w_out
    return y, (k_all, v_all, ki, logf.astype(x.dtype))


def dsa_sample(j, qa, ka, va, qi, wi, ki, cache_k, cache_v, cache_ik, page_table, topk):
    Bd, T = qa.shape[:2]
    n_pages = page_table.shape[1]
    P = n_pages * PAGE_SIZE
    ki_all = jnp.concatenate([cache_ik[j, page_table].reshape(Bd, P, IDX_DIM), ki.astype(cache_ik.dtype)], axis=1)
    qpos = P + jnp.arange(T)
    kpos = jnp.arange(P + T)
    sc = indexer_scores(qi, wi, ki_all)
    sc = jnp.where((kpos[None, :] <= qpos[:, None])[None], sc, -jnp.inf)
    _, idx = lax.top_k(sc, topk)
    bidx = jnp.arange(Bd)[:, None, None]
    is_past = idx < P
    phys = page_table[bidx, jnp.minimum(idx // PAGE_SIZE, n_pages - 1)]
    off = idx % PAGE_SIZE
    new_i = jnp.clip(idx - P, 0, T - 1)
    sel = lambda pool, new: jnp.where(is_past[..., None, None], pool[j, phys, off, :H_A], new[bidx, new_i])
    valid = idx <= qpos[None, :, None]
    return sparse_attend(qa, sel(cache_k, ka), sel(cache_v, va), valid)


def partial_softmax(s, v):
    m = jnp.max(s, axis=-1, keepdims=True)
    e = jnp.exp(s - m)
    return m, jnp.sum(e, axis=-1, keepdims=True), jnp.einsum('bhts,bshd->bhtd', e, v.astype(F32))


def fox_sample(j, qb, kb, vb, logf, cache_k, cache_v, cache_lf, page_table):
    Bd, T = qb.shape[:2]
    n_pages = page_table.shape[1]
    P = n_pages * PAGE_SIZE
    scale = HEAD_DIM ** -0.5
    lf_past = cache_lf[j, page_table].reshape(Bd, P, H_B).astype(F32)
    r = lax.cumsum(lf_past, axis=1, reverse=True) - lf_past
    r = jnp.moveaxis(r.reshape(Bd, n_pages, PAGE_SIZE, H_B), 3, 1)
    cn = jnp.moveaxis(jnp.cumsum(logf, axis=1), 1, 2)

    def page_part(p):
        phys = page_table[:, p]
        kp = cache_k[j, phys, :, H_A:]
        vp = cache_v[j, phys, :, H_A:]
        s = jnp.einsum('bthd,bshd->bhts', qb, kp, preferred_element_type=F32) * scale
        return partial_softmax(s + r[:, :, p][:, :, None, :] + cn[..., None], vp)

    m_p, l_p, o_p = lax.map(page_part, jnp.arange(n_pages))
    s_new = jnp.einsum('bthd,bshd->bhts', qb, kb, preferred_element_type=F32) * scale
    s_new = s_new + cn[..., None] - cn[:, :, None, :]
    tri = jnp.arange(T)[None, :] <= jnp.arange(T)[:, None]
    m_n, l_n, o_n = partial_softmax(jnp.where(tri, s_new, -jnp.inf), vb)
    m_all = jnp.concatenate([m_p, m_n[None]], axis=0)
    l_all = jnp.concatenate([l_p, l_n[None]], axis=0)
    o_all = jnp.concatenate([o_p, o_n[None]], axis=0)
    wgt = jnp.exp(m_all - jnp.max(m_all, axis=0))
    out = jnp.sum(o_all * wgt, axis=0) / jnp.sum(l_all * wgt, axis=0)
    return jnp.moveaxis(out, 1, 2).astype(qb.dtype)


def att_mixer_sample(x, j, cache_k, cache_v, cache_ik, cache_lf, page_table, w_in, b_f, w_out):
    Bd, T, _ = x.shape
    qa, qi, wi, ki, qb, k_all, v_all, logf = att_projections(x, w_in, b_f)
    P = page_table.shape[1] * PAGE_SIZE
    topk = min(TOPK_MAX, (P + T) // 4)
    oa = dsa_sample(j, qa, k_all[:, :, :H_A], v_all[:, :, :H_A], qi, wi, ki, cache_k, cache_v, cache_ik, page_table, topk)
    ob = fox_sample(j, qb, k_all[:, :, H_A:], v_all[:, :, H_A:], logf, cache_k, cache_v, cache_lf, page_table)
    y = jnp.concatenate([oa, ob.astype(oa.dtype)], axis=2).reshape(Bd, T, H_AB * HEAD_DIM) @ w_out
    return y, (k_all, v_all, ki, logf.astype(x.dtype))


def rotary(x, pos):
    half = x.shape[-1] // 2
    inv = ROPE_BASE ** (-jnp.arange(half, dtype=F32) / half)
    ang = pos.astype(F32)[:, None] * inv[None, :]
    cos = jnp.cos(ang)[None, :, None, :]
    sin = jnp.sin(ang)[None, :, None, :]
    x1 = x[..., :half].astype(F32)
    x2 = x[..., half:].astype(F32)
    return jnp.concatenate([x1 * cos - x2 * sin, x1 * sin + x2 * cos], axis=-1).astype(x.dtype)


def retention_chunked(q, k, v, s0):
    B, L = q.shape[:2]
    C = math.gcd(L, RET_CHUNK)
    lg = jnp.log1p(-jnp.exp2(-5.0 - jnp.arange(H_C, dtype=F32)))
    i = jnp.arange(C, dtype=F32)
    diff = i[:, None] - i[None, :]
    d_intra = jnp.where(diff >= 0, jnp.exp(lg[:, None, None] * jnp.maximum(diff, 0.0)), 0.0)
    q_dec = jnp.exp(lg[None, :] * (i[:, None] + 1.0))
    k_dec = jnp.exp(lg[None, :] * (C - 1.0 - i[:, None]))
    c_dec = jnp.exp(lg * C)
    to_chunks = lambda z: jnp.moveaxis(z.astype(F32).reshape(B, L // C, C, *z.shape[2:]), 1, 0)

    def step(S, inp):
        qc, kc, vc = inp
        a = jnp.einsum('bihd,bjhd->bhij', qc, kc) * d_intra
        o = jnp.einsum('bhij,bjhe->bihe', a, vc) + jnp.einsum('bihd,bhde->bihe', qc, S) * q_dec[:, :, None]
        S = S * c_dec[:, None, None] + jnp.einsum('bjhd,bjhe->bhde', kc * k_dec[:, :, None], vc)
        return S, o

    s_fin, o = lax.scan(step, s0.astype(F32), (to_chunks(q), to_chunks(k), to_chunks(v)))
    return jnp.moveaxis(o, 0, 1).reshape(B, L, H_C, DV_C), s_fin


def retention_mixer(x, pos, s0, w_in, gn_g, w_out):
    B, T, _ = x.shape
    q, k, v, g = split_cols(x @ w_in, SPLIT_RET)
    q = rotary(q.reshape(B, T, H_C, DK_C), pos) * (DK_C ** -0.5)
    k = rotary(k.reshape(B, T, H_C, DK_C), pos)
    o, s_new = retention_chunked(q, k, v.reshape(B, T, H_C, DV_C), s0)
    mu = jnp.mean(o, axis=-1, keepdims=True)
    var = jnp.mean(jnp.square(o - mu), axis=-1, keepdims=True)
    o = (o - mu) * lax.rsqrt(var + LN_EPS) * gn_g
    gated = (jax.nn.silu(g.reshape(B, T, H_C, DV_C).astype(F32)) * o).astype(x.dtype)
    return gated.reshape(B, T, H_C * DV_C) @ w_out, s_new


def setup_inputs(seed: int = 0) -> dict:
    key = jax.random.key(seed)
    ks = iter(jax.random.split(key, 32))
    nrm = lambda shape, scale=1.0: jax.random.normal(next(ks), shape, F32) * scale
    n_pages = PAST_LEN // PAGE_SIZE
    n_used = DEC_BATCH * n_pages
    n_pool = n_used + n_used // 4
    x_prompt = nrm((BATCH, SEQ, D_MODEL))
    x_sample = nrm((DEC_BATCH, DEC_SEQ, D_MODEL))
    cache_k = nrm((N_ATT_LAYERS, n_pool, PAGE_SIZE, H_AB, HEAD_DIM))
    cache_v = nrm((N_ATT_LAYERS, n_pool, PAGE_SIZE, H_AB, HEAD_DIM))
    cache_idx_k = nrm((N_ATT_LAYERS, n_pool, PAGE_SIZE, IDX_DIM))
    cache_logf = jax.nn.log_sigmoid(FORGET_BIAS + nrm((N_ATT_LAYERS, n_pool, PAGE_SIZE, H_B)))
    gamma = 1.0 - jnp.exp2(-5.0 - jnp.arange(H_C, dtype=F32))
    ret_scale = jnp.sqrt(jnp.minimum(float(PAST_LEN), 1.0 / (1.0 - gamma ** 2)))
    state_ret = nrm((N_RET_LAYERS, DEC_BATCH, H_C, DK_C, DV_C)) * ret_scale[:, None, None]
    perm = jax.random.permutation(next(ks), n_pool)[:n_used]
    page_table = perm.reshape(DEC_BATCH, n_pages).astype(jnp.int32)
    w_in_att = nrm((N_ATT_LAYERS, D_MODEL, D_IN_ATT), D_MODEL ** -0.5)
    b_forget = FORGET_BIAS + nrm((N_ATT_LAYERS, H_B), 0.1)
    w_out_att = nrm((N_ATT_LAYERS, H_AB * HEAD_DIM, D_MODEL), BETA * (H_AB * HEAD_DIM) ** -0.5)
    w_in_ret = nrm((N_RET_LAYERS, D_MODEL, D_IN_RET), D_MODEL ** -0.5)
    gn_ret = 1.0 + nrm((N_RET_LAYERS, H_C, DV_C), 0.02)
    w_out_ret = nrm((N_RET_LAYERS, H_C * DV_C, D_MODEL), BETA * (H_C * DV_C) ** -0.5)
    ln_mix_g = 1.0 + nrm((DEPTH, D_MODEL), 0.02)
    ln_mix_b = nrm((DEPTH, D_MODEL), 0.02)
    ln_ffn_g = 1.0 + nrm((DEPTH, D_MODEL), 0.02)
    ln_ffn_b = nrm((DEPTH, D_MODEL), 0.02)
    w_ffn_up = nrm((DEPTH, D_MODEL, D_FF), D_MODEL ** -0.5)
    w_ffn_down = nrm((DEPTH, D_FF, D_MODEL), BETA * D_FF ** -0.5)
    return {'x_prompt': x_prompt, 'x_sample': x_sample, 'cache_k': cache_k, 'cache_v': cache_v,
            'cache_idx_k': cache_idx_k, 'cache_logf': cache_logf, 'state_ret': state_ret, 'page_table': page_table,
            'w_in_att': w_in_att, 'b_forget': b_forget, 'w_out_att': w_out_att, 'w_in_ret': w_in_ret,
            'gn_ret': gn_ret, 'w_out_ret': w_out_ret, 'ln_mix_g': ln_mix_g, 'ln_mix_b': ln_mix_b,
            'ln_ffn_g': ln_ffn_g, 'ln_ffn_b': ln_ffn_b, 'w_ffn_up': w_ffn_up, 'w_ffn_down': w_ffn_down}


def reference(x_prompt, x_sample, cache_k, cache_v, cache_idx_k, cache_logf, state_ret, page_table,
              w_in_att, b_forget, w_out_att, w_in_ret, gn_ret, w_out_ret,
              ln_mix_g, ln_mix_b, ln_ffn_g, ln_ffn_b, w_ffn_up, w_ffn_down):
    xp, xs = x_prompt, x_sample
    Bp, S = xp.shape[:2]
    T = xs.shape[1]
    P = page_table.shape[1] * PAGE_SIZE
    pos_p = jnp.arange(S)
    pos_s = P + jnp.arange(T)
    rows_p = ([], [], [], [])
    rows_s = ([], [], [], [])
    ret_p, ret_s = [], []
    for layer in range(DEPTH):
        j = layer // 2
        if layer % 2 == 0:
            yp, new_p = att_mixer_prompt(xp, w_in_att[j], b_forget[j], w_out_att[j])
            ys, new_s = att_mixer_sample(xs, j, cache_k, cache_v, cache_idx_k, cache_logf, page_table,
                                         w_in_att[j], b_forget[j], w_out_att[j])
            for lst, arr in zip(rows_p, new_p):
                lst.append(arr)
            for lst, arr in zip(rows_s, new_s):
                lst.append(arr)
        else:
            s0 = jnp.zeros((Bp, H_C, DK_C, DV_C), F32)
            yp, sp = retention_mixer(xp, pos_p, s0, w_in_ret[j], gn_ret[j], w_out_ret[j])
            ys, ss = retention_mixer(xs, pos_s, state_ret[j], w_in_ret[j], gn_ret[j], w_out_ret[j])
            ret_p.append(sp.astype(xp.dtype))
            ret_s.append(ss.astype(state_ret.dtype))
        xp = layer_norm(ALPHA * xp + yp, ln_mix_g[layer], ln_mix_b[layer])
        xs = layer_norm(ALPHA * xs + ys, ln_mix_g[layer], ln_mix_b[layer])
        xp = layer_norm(ALPHA * xp + sq_relu_mlp(xp, w_ffn_up[layer], w_ffn_down[layer]), ln_ffn_g[layer], ln_ffn_b[layer])
        xs = layer_norm(ALPHA * xs + sq_relu_mlp(xs, w_ffn_up[layer], w_ffn_down[layer]), ln_ffn_g[layer], ln_ffn_b[layer])
    new_k_p = jnp.stack(rows_p[0])
    new_v_p = jnp.stack(rows_p[1])
    new_ik_p = jnp.stack(rows_p[2])
    new_lf_p = jnp.stack(rows_p[3])
    new_ret_p = jnp.stack(ret_p)
    new_k_s = jnp.stack(rows_s[0])
    new_v_s = jnp.stack(rows_s[1])
    new_ik_s = jnp.stack(rows_s[2])
    new_lf_s = jnp.stack(rows_s[3])
    new_ret_s = jnp.stack(ret_s)
    return (xp, xs, new_k_p, new_v_p, new_ik_p, new_lf_p, new_ret_p, new_k_s, new_v_s, new_ik_s, new_lf_s, new_ret_s)
```

```python
import functools
import math

import jax
import jax.numpy as jnp
import numpy as np
from jax import lax
from jax.experimental import pallas as pl
from jax.experimental.pallas import tpu as pltpu

F32 = jnp.float32
BF16 = jnp.bfloat16

HEAD_DIM = 128
N_HEADS_A = 16
N_HEADS_B = 16
IDX_HEADS = 64
IDX_DIM = 128
IDX_SCALE = (IDX_HEADS * IDX_DIM) ** -0.5
TOPK_MAX = 256
PAGE_SIZE = 128
DK_C = 256
DV_C = 512
N_HEADS_C = 16
RET_CHUNK = 128
ROPE_BASE = 10000.0
DEPTH = 2
ALPHA = (2 * DEPTH) ** 0.25
LN_EPS = 1e-5
ATT_SCALE = HEAD_DIM ** -0.5

LANES = 128
SUBLANES = 8
VMEM_LIMIT_BYTES = 56 * 1024 * 1024
NEG_BIG = -1e30
INT_MIN = -2 ** 31


def _cparams(sem):
    return pltpu.CompilerParams(dimension_semantics=sem, vmem_limit_bytes=VMEM_LIMIT_BYTES)


def _mm_kernel(x_ref, w_ref, *refs, nk, act, n_out):
    outs = refs[:n_out]

    def finish(acc):
        if act == "relu2":
            r = jnp.maximum(acc, 0.0)
            acc = r * r
        for o in outs:
            if len(o.shape) == 3:
                for hh in range(o.shape[0]):
                    o[hh] = acc[:, hh * LANES:(hh + 1) * LANES].astype(o.dtype)
            else:
                o[...] = acc.astype(o.dtype)

    if nk == 1:
        finish(jnp.dot(x_ref[...], w_ref[...], preferred_element_type=F32))
    else:
        acc_ref = refs[n_out]
        k = pl.program_id(2)

        @pl.when(k == 0)
        def _():
            acc_ref[...] = jnp.zeros_like(acc_ref)

        acc_ref[...] += jnp.dot(x_ref[...], w_ref[...], preferred_element_type=F32)

        @pl.when(k == nk - 1)
        def _():
            finish(acc_ref[...])


def matmul(x, w, out_dtypes, *, bm, bn, bk=None, act=None, head_major=False):
    M, K = x.shape
    K2, N = w.shape
    assert K == K2
    bk = bk or K
    bm = min(bm, M)
    bn = min(bn, N)
    assert M % bm == 0 and N % bn == 0 and K % bk == 0
    nk = K // bk
    grid = (N // bn, M // bm, nk)
    in_specs = [pl.BlockSpec((bm, bk), lambda j, i, k: (i, k)),
                pl.BlockSpec((bk, bn), lambda j, i, k: (k, j))]
    if head_major:
        out_specs = [pl.BlockSpec((bn // LANES, bm, LANES), lambda j, i, k: (j, i, 0)) for _ in out_dtypes]
        out_shape = [jax.ShapeDtypeStruct((N // LANES, M, LANES), d) for d in out_dtypes]
    else:
        out_specs = [pl.BlockSpec((bm, bn), lambda j, i, k: (i, j)) for _ in out_dtypes]
        out_shape = [jax.ShapeDtypeStruct((M, N), d) for d in out_dtypes]
    scratch = [pltpu.VMEM((bm, bn), F32)] if nk > 1 else []
    res = pl.pallas_call(
        functools.partial(_mm_kernel, nk=nk, act=act, n_out=len(out_dtypes)),
        grid=grid, in_specs=in_specs, out_specs=out_specs, out_shape=out_shape,
        scratch_shapes=scratch,
        compiler_params=_cparams(("parallel", "parallel", "arbitrary")),
        name="matmul",
    )(x, w)
    return res


def _ln_kernel(x_ref, y_ref, g_ref, b_ref, of_ref, ob_ref):
    z = ALPHA * x_ref[...] + y_ref[...]
    mu = jnp.mean(z, axis=-1, keepdims=True)
    zc = z - mu
    var = jnp.mean(zc * zc, axis=-1, keepdims=True)
    r = zc * lax.rsqrt(var + LN_EPS) * g_ref[...] + b_ref[...]
    of_ref[...] = r
    ob_ref[...] = r.astype(BF16)


def ln_residual(x, y, g, b, *, bm=128):
    M, D = x.shape
    bm = min(bm, M)
    row = pl.BlockSpec((bm, D), lambda i: (i, 0))
    vec = pl.BlockSpec((1, D), lambda i: (0, 0))
    return pl.pallas_call(
        _ln_kernel, grid=(M // bm,), in_specs=[row, row, vec, vec], out_specs=[row, row],
        out_shape=[jax.ShapeDtypeStruct((M, D), F32), jax.ShapeDtypeStruct((M, D), BF16)],
        compiler_params=_cparams(("parallel",)), name="ln_residual",
    )(x, y, g.reshape(1, D), b.reshape(1, D))


def _attend_heads(q_ref, k_ref, v_ref, o_ref, bias_fn, n_heads):
    for h in range(n_heads):
        cs = slice(h * HEAD_DIM, (h + 1) * HEAD_DIM)
        s = lax.dot_general(q_ref[:, cs], k_ref[:, cs], (((1,), (1,)), ((), ())),
                            preferred_element_type=F32) * ATT_SCALE + bias_fn(h)
        m = jnp.max(s, axis=-1, keepdims=True)
        p = jnp.exp(s - m)
        l = jnp.sum(p, axis=-1, keepdims=True)
        o = jnp.dot(p.astype(BF16), v_ref[:, cs], preferred_element_type=F32) / l
        o_ref[:, cs] = o.astype(o_ref.dtype)


def _dsa_prompt_kernel(qi_ref, wi_ref, kit_ref, q_ref, k_ref, v_ref, o_ref, sc_ref, key_ref, wb_ref,
                       *, tq, s_len, topk):
    i = pl.program_id(1)
    w = wi_ref[:, :IDX_HEADS] * IDX_SCALE
    for h in range(IDX_HEADS):
        wb_ref[h] = jnp.broadcast_to(w[:, h:h + 1], (tq, LANES))
    sc_ref[...] = jnp.zeros_like(sc_ref)

    def head_body(h, carry):
        s = jnp.dot(qi_ref[h], kit_ref[0], preferred_element_type=F32)
        wfull = jnp.concatenate([wb_ref[h]] * (s_len // LANES), axis=1)
        sc_ref[...] += jnp.maximum(s, 0.0) * wfull
        return carry

    lax.fori_loop(0, IDX_HEADS, head_body, 0)

    qpos = i * tq + lax.broadcasted_iota(jnp.int32, (tq, 1), 0)
    kpos = lax.broadcasted_iota(jnp.int32, (1, s_len), 1)
    causal = kpos <= qpos
    sc = jnp.where(causal, sc_ref[...], -jnp.inf)
    bits = lax.bitcast_convert_type(sc, jnp.int32)
    key_ref[...] = jnp.where(bits < 0, bits ^ jnp.int32(0x7FFFFFFF), bits)

    def radix_body(it, u):
        cand_u = u | jnp.left_shift(jnp.int32(1), 31 - it)
        cand = cand_u ^ jnp.int32(INT_MIN)
        cnt = jnp.sum(jnp.where(key_ref[...] >= cand, 1.0, 0.0), axis=-1, keepdims=True)
        return jnp.where(cnt >= float(topk), cand_u, u)

    u = lax.fori_loop(0, 32, radix_body, jnp.zeros((tq, 1), jnp.int32))
    thr = u ^ jnp.int32(INT_MIN)
    sel = jnp.logical_and(key_ref[...] >= thr, causal)
    sc_ref[...] = jnp.where(sel, 0.0, NEG_BIG)

    _attend_heads(q_ref, k_ref, v_ref, o_ref, lambda h: sc_ref[...], N_HEADS_A)


def dsa_prompt(qi_hm, small, kit, q, k, v, *, batch, s_len, tq=128):
    nq = s_len // tq
    wa = N_HEADS_A * HEAD_DIM
    topk = min(TOPK_MAX, s_len // 4)
    kv_spec = pl.BlockSpec((s_len, wa), lambda b, i: (b, 0), pipeline_mode=pl.Buffered(1))
    return pl.pallas_call(
        functools.partial(_dsa_prompt_kernel, tq=tq, s_len=s_len, topk=topk),
        grid=(batch, nq),
        in_specs=[pl.BlockSpec((IDX_HEADS, tq, IDX_DIM), lambda b, i: (0, b * nq + i, 0)),
                  pl.BlockSpec((tq, LANES), lambda b, i: (b * nq + i, 1)),
                  pl.BlockSpec((1, IDX_DIM, s_len), lambda b, i: (b, 0, 0)),
                  pl.BlockSpec((tq, wa), lambda b, i: (b * nq + i, 0)),
                  kv_spec, kv_spec],
        out_specs=pl.BlockSpec((tq, wa), lambda b, i: (b * nq + i, 0)),
        out_shape=jax.ShapeDtypeStruct((batch * s_len, wa), BF16),
        scratch_shapes=[pltpu.VMEM((tq, s_len), F32), pltpu.VMEM((tq, s_len), jnp.int32),
                        pltpu.VMEM((IDX_HEADS, tq, LANES), F32)],
        compiler_params=_cparams(("parallel", "arbitrary")), name="dsa_prompt",
    )(qi_hm, small, kit, q, k, v)


def _cumsum_lanes(x):
    n = x.shape[-1]
    lane = lax.broadcasted_iota(jnp.int32, x.shape, len(x.shape) - 1)
    sh = 1
    while sh < n:
        x = x + jnp.where(lane >= sh, pltpu.roll(x, sh, len(x.shape) - 1), 0.0)
        sh *= 2
    return x


def _fox_prep_kernel(fb_ref, bf_ref, lf_ref, c_ref):
    logf = jax.nn.log_sigmoid(fb_ref[0] + bf_ref[...])
    lf_ref[0] = logf
    c_ref[0] = _cumsum_lanes(logf)


def fox_prep(fbt, b_f):
    B, H, T = fbt.shape
    blk = pl.BlockSpec((1, H, T), lambda b: (b, 0, 0))
    return pl.pallas_call(
        _fox_prep_kernel, grid=(B,),
        in_specs=[blk, pl.BlockSpec((H, 1), lambda b: (0, 0))],
        out_specs=[blk, blk],
        out_shape=[jax.ShapeDtypeStruct((B, H, T), F32)] * 2,
        compiler_params=_cparams(("parallel",)), name="fox_prep",
    )(fbt, b_f.reshape(H, 1))


def _fox_prompt_kernel(c_ref, q_ref, k_ref, v_ref, o_ref, *, tq, s_len):
    i = pl.program_id(1)
    qpos = i * tq + lax.broadcasted_iota(jnp.int32, (tq, 1), 0)
    kpos = lax.broadcasted_iota(jnp.int32, (1, s_len), 1)
    causal_bias = jnp.where(kpos <= qpos, 0.0, NEG_BIG)

    def bias(h):
        return causal_bias - c_ref[0, h:h + 1, :]

    _attend_heads(q_ref, k_ref, v_ref, o_ref, bias, N_HEADS_B)


def fox_prompt(c, q, k, v, *, batch, s_len, tq=128):
    nq = s_len // tq
    wb = N_HEADS_B * HEAD_DIM
    kv_spec = pl.BlockSpec((s_len, wb), lambda b, i: (b, 1), pipeline_mode=pl.Buffered(1))
    return pl.pallas_call(
        functools.partial(_fox_prompt_kernel, tq=tq, s_len=s_len),
        grid=(batch, nq),
        in_specs=[pl.BlockSpec((1, N_HEADS_B, s_len), lambda b, i: (b, 0, 0)),
                  pl.BlockSpec((tq, wb), lambda b, i: (b * nq + i, 1)),
                  kv_spec, kv_spec],
        out_specs=pl.BlockSpec((tq, wb), lambda b, i: (b * nq + i, 0)),
        out_shape=jax.ShapeDtypeStruct((batch * s_len, wb), BF16),
        compiler_params=_cparams(("parallel", "arbitrary")), name="fox_prompt",
    )(c, q, k, v)


def _samp_scores_kernel(pt_ref, cache_ref, kin_ref, qi_ref, wsb_ref, out_ref, *, n_pages, t_new):
    p = pl.program_id(1)

    def scores(kp):
        s = lax.dot_general(qi_ref[0], kp.astype(BF16), (((1,), (1,)), ((), ())),
                            preferred_element_type=F32)
        r = jnp.maximum(s, 0.0) * wsb_ref[0]
        acc = r[0:t_new]
        for h in range(1, IDX_HEADS):
            acc = acc + r[h * t_new:(h + 1) * t_new]
        return acc

    @pl.when(p < n_pages)
    def _():
        out_ref[0] = scores(cache_ref[0])

    @pl.when(p == n_pages)
    def _():
        sc = scores(kin_ref[0])
        t = lax.broadcasted_iota(jnp.int32, sc.shape, 0)
        s_new = lax.broadcasted_iota(jnp.int32, sc.shape, 1)
        out_ref[0] = jnp.where(s_new <= t, sc, -jnp.inf)


def samp_scores(page_table, cache_ik, ki_new_pad, qi_s, wsb):
    Bd, n_pages = page_table.shape
    rows = qi_s.shape[1]
    t_new = rows // IDX_HEADS
    grid_spec = pltpu.PrefetchScalarGridSpec(
        num_scalar_prefetch=1, grid=(Bd, n_pages + 1),
        in_specs=[pl.BlockSpec((1, PAGE_SIZE, IDX_DIM),
                               lambda b, p, pt: (pt[b, jnp.minimum(p, n_pages - 1)], 0, 0)),
                  pl.BlockSpec((1, PAGE_SIZE, IDX_DIM), lambda b, p, pt: (b, 0, 0)),
                  pl.BlockSpec((1, rows, IDX_DIM), lambda b, p, pt: (b, 0, 0)),
                  pl.BlockSpec((1, rows, LANES), lambda b, p, pt: (b, 0, 0))],
        out_specs=pl.BlockSpec((1, t_new, PAGE_SIZE), lambda b, p, pt: (b, 0, p)))
    return pl.pallas_call(
        functools.partial(_samp_scores_kernel, n_pages=n_pages, t_new=t_new),
        grid_spec=grid_spec,
        out_shape=jax.ShapeDtypeStruct((Bd, t_new, (n_pages + 1) * PAGE_SIZE), F32),
        compiler_params=_cparams(("parallel", "arbitrary")), name="samp_scores",
    )(page_table, cache_ik, ki_new_pad, qi_s, wsb)


def _samp_select_kernel(sc_ref, thr_ref, key_ref, *, topk):
    sc = sc_ref[0]
    bits = lax.bitcast_convert_type(sc, jnp.int32)
    key_ref[...] = jnp.where(bits < 0, bits ^ jnp.int32(0x7FFFFFFF), bits)
    rows = sc.shape[0]

    def radix_body(it, u):
        cand_u = u | jnp.left_shift(jnp.int32(1), 31 - it)
        cand = cand_u ^ jnp.int32(INT_MIN)
        cnt = jnp.sum(jnp.where(key_ref[...] >= cand, 1.0, 0.0), axis=-1, keepdims=True)
        return jnp.where(cnt >= float(topk), cand_u, u)

    u = lax.fori_loop(0, 32, radix_body, jnp.zeros((rows, 1), jnp.int32))
    key = u ^ jnp.int32(INT_MIN)
    tbits = jnp.where(key < 0, key ^ jnp.int32(0x7FFFFFFF), key)
    thr = lax.bitcast_convert_type(tbits, F32)
    thr_ref[0] = jnp.broadcast_to(thr, (rows, LANES))


def samp_select(sc, topk):
    Bd, T, L = sc.shape
    return pl.pallas_call(
        functools.partial(_samp_select_kernel, topk=topk), grid=(Bd,),
        in_specs=[pl.BlockSpec((1, T, L), lambda b: (b, 0, 0))],
        out_specs=pl.BlockSpec((1, T, LANES), lambda b: (b, 0, 0)),
        out_shape=jax.ShapeDtypeStruct((Bd, T, LANES), F32),
        scratch_shapes=[pltpu.VMEM((T, L), jnp.int32)],
        compiler_params=_cparams(("parallel",)), name="samp_select",
    )(sc)


def _samp_attend_kernel(pt_ref, ck_ref, cv_ref, clf_ref, sc_ref, thr_ref, qa_ref, qb_ref,
                        kn_ref, vn_ref, lfn_ref, oa_ref, ob_ref, m_ref, l_ref, acc_ref, carry_ref,
                        *, n_pages, t_new):
    p = pl.program_id(1)
    n_rows = N_HEADS_A * t_new
    wa = N_HEADS_A * HEAD_DIM

    @pl.when(p == 0)
    def _():
        m_ref[...] = jnp.full_like(m_ref, NEG_BIG)
        l_ref[...] = jnp.zeros_like(l_ref)
        acc_ref[...] = jnp.zeros_like(acc_ref)
        carry_ref[...] = jnp.zeros_like(carry_ref)

    def expand_heads(x):
        return jnp.concatenate(
            [jnp.broadcast_to(x[h:h + 1, :], (t_new, x.shape[1])) for h in range(N_HEADS_B)], axis=0)

    def update(g, q_ref, k_all, v_all, logits_fn, mask):
        kg = k_all[:, g * wa:(g + 1) * wa].astype(BF16)
        vg = v_all[:, g * wa:(g + 1) * wa].astype(BF16)
        s = lax.dot_general(q_ref[0], kg, (((1,), (1,)), ((), ())), preferred_element_type=F32) * ATT_SCALE
        s = jnp.where(mask, logits_fn(s), NEG_BIG)
        m_old = m_ref[g]
        m_new = jnp.maximum(m_old, jnp.max(s, axis=-1, keepdims=True))
        pe = jnp.where(mask, jnp.exp(s - m_new), 0.0)
        alpha = jnp.exp(m_old - m_new)
        l_ref[g] = alpha * l_ref[g] + jnp.sum(pe, axis=-1, keepdims=True)
        res = jnp.dot(pe.astype(BF16), vg, preferred_element_type=F32)
        diag = jnp.concatenate(
            [res[h * t_new:(h + 1) * t_new, h * HEAD_DIM:(h + 1) * HEAD_DIM] for h in range(N_HEADS_A)], axis=0)
        acc_ref[g] = alpha * acc_ref[g] + diag
        m_ref[g] = m_new

    def sel_mask():
        sel = jnp.where(sc_ref[0] >= thr_ref[0], 1.0, 0.0)
        return jnp.concatenate([sel] * N_HEADS_A, axis=0) > 0.5

    @pl.when(p == 0)
    def _():
        row_t = lax.broadcasted_iota(jnp.int32, (n_rows, PAGE_SIZE), 0) % t_new
        s_new = lax.broadcasted_iota(jnp.int32, (n_rows, PAGE_SIZE), 1)
        causal = s_new <= row_t
        k_all = kn_ref[0]
        v_all = vn_ref[0]
        update(0, qa_ref, k_all, v_all, lambda s: s, jnp.logical_and(sel_mask(), causal))
        cn = expand_heads(_cumsum_lanes(lfn_ref[0]))
        update(1, qb_ref, k_all, v_all, lambda s: s - cn, causal)

    @pl.when(p > 0)
    def _():
        k_all = ck_ref[0]
        v_all = cv_ref[0]
        update(0, qa_ref, k_all, v_all, lambda s: s, sel_mask())
        lf = clf_ref[0]
        pre = _cumsum_lanes(lf)
        tot = pre[:, PAGE_SIZE - 1:PAGE_SIZE]
        r = expand_heads(tot - pre) + carry_ref[...]
        carry_ref[...] += expand_heads(jnp.broadcast_to(tot, (N_HEADS_B, LANES)))[:, :1]
        update(1, qb_ref, k_all, v_all, lambda s: s + r, jnp.full((n_rows, PAGE_SIZE), True))

    @pl.when(p == n_pages)
    def _():
        oa_ref[0] = acc_ref[0] / l_ref[0]
        ob_ref[0] = acc_ref[1] / l_ref[1]


def samp_attend(page_table, cache_k, cache_v, cache_lft, sc, thr, qbd_a, qbd_b, kn_pad, vn_pad, lfnt_pad):
    Bd, n_pages = page_table.shape
    n_rows = qbd_a.shape[1]
    t_new = n_rows // N_HEADS_A
    width = cache_k.shape[-1]

    def page(b, p, pt):
        return pt[b, n_pages - jnp.maximum(p, 1)]

    def sc_col(b, p, pt):
        return jnp.where(p == 0, n_pages, n_pages - jnp.maximum(p, 1))

    grid_spec = pltpu.PrefetchScalarGridSpec(
        num_scalar_prefetch=1, grid=(Bd, n_pages + 1),
        in_specs=[pl.BlockSpec((1, PAGE_SIZE, width), lambda b, p, pt: (page(b, p, pt), 0, 0)),
                  pl.BlockSpec((1, PAGE_SIZE, width), lambda b, p, pt: (page(b, p, pt), 0, 0)),
                  pl.BlockSpec((1, N_HEADS_B, PAGE_SIZE), lambda b, p, pt: (page(b, p, pt), 0, 0)),
                  pl.BlockSpec((1, t_new, PAGE_SIZE), lambda b, p, pt: (b, 0, sc_col(b, p, pt))),
                  pl.BlockSpec((1, t_new, LANES), lambda b, p, pt: (b, 0, 0)),
                  pl.BlockSpec((1, n_rows, width // 2), lambda b, p, pt: (b, 0, 0)),
                  pl.BlockSpec((1, n_rows, width // 2), lambda b, p, pt: (b, 0, 0)),
                  pl.BlockSpec((1, PAGE_SIZE, width), lambda b, p, pt: (b, 0, 0)),
                  pl.BlockSpec((1, PAGE_SIZE, width), lambda b, p, pt: (b, 0, 0)),
                  pl.BlockSpec((1, N_HEADS_B, PAGE_SIZE), lambda b, p, pt: (b, 0, 0))],
        out_specs=[pl.BlockSpec((1, n_rows, HEAD_DIM), lambda b, p, pt: (b, 0, 0))] * 2,
        scratch_shapes=[pltpu.VMEM((2, n_rows, 1), F32), pltpu.VMEM((2, n_rows, 1), F32),
                        pltpu.VMEM((2, n_rows, HEAD_DIM), F32), pltpu.VMEM((n_rows, 1), F32)])
    return pl.pallas_call(
        functools.partial(_samp_attend_kernel, n_pages=n_pages, t_new=t_new),
        grid_spec=grid_spec,
        out_shape=[jax.ShapeDtypeStruct((Bd, n_rows, HEAD_DIM), F32)] * 2,
        compiler_params=_cparams(("parallel", "arbitrary")), name="samp_attend",
    )(page_table, cache_k, cache_v, cache_lft, sc, thr, qbd_a, qbd_b, kn_pad, vn_pad, lfnt_pad)


def _rope_kernel(inv_ref, cos_ref, sin_ref, *, pos0):
    n = cos_ref.shape[0]
    pos = (pos0 + lax.broadcasted_iota(jnp.int32, (n, LANES), 0)).astype(F32)
    ang = pos * inv_ref[...]
    cos_ref[...] = jnp.cos(ang)
    sin_ref[...] = jnp.sin(ang)


def rope_tables(pos0, n):
    half = DK_C // 2
    inv = (ROPE_BASE ** (-jnp.arange(half, dtype=F32) / half)).reshape(1, half)
    return pl.pallas_call(
        functools.partial(_rope_kernel, pos0=pos0),
        out_shape=[jax.ShapeDtypeStruct((n, half), F32)] * 2, name="rope_tables",
    )(inv)


def _retention_kernel(*refs, chunk, n_chunks, has_s0):
    if has_s0:
        q_ref, k_ref, v_ref, g_ref, cos_ref, sin_ref, lg_ref, gn_ref, s0_ref, o_ref, sout_ref, st_ref = refs
        st_ref[...] = s0_ref[0, 0]
    else:
        q_ref, k_ref, v_ref, g_ref, cos_ref, sin_ref, lg_ref, gn_ref, o_ref, sout_ref, st_ref = refs
        st_ref[...] = jnp.zeros_like(st_ref)
    half = DK_C // 2
    lg = lg_ref[0][:, :1]
    i_col = lax.broadcasted_iota(jnp.int32, (chunk, 1), 0).astype(F32)
    j_row = lax.broadcasted_iota(jnp.int32, (1, chunk), 1).astype(F32)
    diff = i_col - j_row
    d_intra = jnp.where(diff >= 0, jnp.exp(lg * jnp.maximum(diff, 0.0)), 0.0)
    q_dec = jnp.exp(lg * (i_col + 1.0))
    k_dec = jnp.exp(lg * (chunk - 1.0 - i_col))
    c_dec = jnp.exp(lg * chunk)
    gn = gn_ref[0]

    def rot(x, cos, sin):
        x1, x2 = x[:, :half], x[:, half:]
        return jnp.concatenate([x1 * cos - x2 * sin, x1 * sin + x2 * cos], axis=1)

    def chunk_body(c, carry):
        rows = pl.ds(pl.multiple_of(c * chunk, chunk), chunk)
        cos, sin = cos_ref[rows, :], sin_ref[rows, :]
        qr = rot(q_ref[rows, :], cos, sin) * (DK_C ** -0.5)
        kr = rot(k_ref[rows, :], cos, sin)
        qb = qr.astype(BF16)
        vb = v_ref[rows, :].astype(BF16)
        a = lax.dot_general(qb, kr.astype(BF16), (((1,), (1,)), ((), ())),
                            preferred_element_type=F32) * d_intra
        st = st_ref[...]
        o = (jnp.dot(a.astype(BF16), vb, preferred_element_type=F32)
             + jnp.dot(qb, st.astype(BF16), preferred_element_type=F32) * q_dec)
        kd = (kr * k_dec).astype(BF16)
        st_ref[...] = st * c_dec + lax.dot_general(kd, vb, (((0,), (0,)), ((), ())),
                                                   preferred_element_type=F32)
        mu = jnp.mean(o, axis=-1, keepdims=True)
        oc = o - mu
        var = jnp.mean(oc * oc, axis=-1, keepdims=True)
        on = oc * lax.rsqrt(var + LN_EPS) * gn
        g = g_ref[rows, :]
        o_ref[rows, :] = (g * jax.nn.sigmoid(g) * on).astype(o_ref.dtype)
        return carry

    if n_chunks == 1:
        chunk_body(0, 0)
    else:
        lax.fori_loop(0, n_chunks, chunk_body, 0)
    sout_ref[0, 0] = st_ref[...]


def retention(zq, zk, zv, zg, cos, sin, lg_tab, gn, s0, *, batch, t_len, out_dtype):
    chunk = math.gcd(t_len, RET_CHUNK)
    n_chunks = t_len // chunk
    H = N_HEADS_C
    in_specs = [pl.BlockSpec((t_len, DK_C), lambda b, h: (b, h)),
                pl.BlockSpec((t_len, DK_C), lambda b, h: (b, h)),
                pl.BlockSpec((t_len, DV_C), lambda b, h: (b, h)),
                pl.BlockSpec((t_len, DV_C), lambda b, h: (b, h)),
                pl.BlockSpec((t_len, DK_C // 2), lambda b, h: (0, 0)),
                pl.BlockSpec((t_len, DK_C // 2), lambda b, h: (0, 0)),
                pl.BlockSpec((1, 1, LANES), lambda b, h: (h, 0, 0)),
                pl.BlockSpec((1, 1, DV_C), lambda b, h: (h, 0, 0))]
    args = [zq, zk, zv, zg, cos, sin, lg_tab, gn]
    if s0 is not None:
        in_specs.append(pl.BlockSpec((1, 1, DK_C, DV_C), lambda b, h: (b, h, 0, 0)))
        args.append(s0)
    return pl.pallas_call(
        functools.partial(_retention_kernel, chunk=chunk, n_chunks=n_chunks, has_s0=s0 is not None),
        grid=(batch, H), in_specs=in_specs,
        out_specs=[pl.BlockSpec((t_len, DV_C), lambda b, h: (b, h)),
                   pl.BlockSpec((1, 1, DK_C, DV_C), lambda b, h: (b, h, 0, 0))],
        out_shape=[jax.ShapeDtypeStruct((batch * t_len, H * DV_C), out_dtype),
                   jax.ShapeDtypeStruct((batch, H, DK_C, DV_C), F32)],
        scratch_shapes=[pltpu.VMEM((DK_C, DV_C), F32)],
        compiler_params=_cparams(("parallel", "parallel")), name="retention",
    )(*args)


def _mm_blocks(m):
    return dict(bm=512, bn=1024) if m >= 512 else dict(bm=m, bn=1024)


def _ffn(x_f32, x_bf, w_up, w_down, g, b):
    m = x_bf.shape[0]
    (h,) = matmul(x_bf, w_up, [BF16], act="relu2", **_mm_blocks(m))
    (y,) = matmul(h, w_down, [F32], bk=4096, **_mm_blocks(m))
    return ln_residual(x_f32, y, g, b)


def _att_projections(x_bf, w_q, w_k, w_v, w_qi, w_sm):
    m = x_bf.shape[0]
    blk = _mm_blocks(m)
    (q,) = matmul(x_bf, w_q, [BF16], **blk)
    kf, kb = matmul(x_bf, w_k, [F32, BF16], **blk)
    vf, vb = matmul(x_bf, w_v, [F32, BF16], **blk)
    (qi,) = matmul(x_bf, w_qi, [BF16], head_major=True, **blk)
    (small,) = matmul(x_bf, w_sm, [F32], **blk)
    return q, kf, kb, vf, vb, qi, small


def kernel(x_prompt, x_sample, cache_k, cache_v, cache_idx_k, cache_logf, state_ret, page_table, w_in_att, b_forget, w_out_att, w_in_ret, gn_ret, w_out_ret, ln_mix_g, ln_mix_b, ln_ffn_g, ln_ffn_b, w_ffn_up, w_ffn_down):
    Bp, S, D = x_prompt.shape
    Bd, T, _ = x_sample.shape
    n_pages = page_table.shape[1]
    P = n_pages * PAGE_SIZE
    n_pool = cache_k.shape[1]
    wa = N_HEADS_A * HEAD_DIM
    wab = wa + N_HEADS_B * HEAD_DIM

    split_att = (wa, wa, wa, IDX_HEADS * IDX_DIM, IDX_DIM, IDX_HEADS, wa, wa, wa, N_HEADS_B)
    off = np.concatenate([[0], np.cumsum(split_att)])
    w0 = w_in_att[0]
    seg = lambda i: w0[:, off[i]:off[i + 1]]
    w_q = jnp.concatenate([seg(0), seg(6)], axis=1).astype(BF16)
    w_k = jnp.concatenate([seg(1), seg(7)], axis=1).astype(BF16)
    w_v = jnp.concatenate([seg(2), seg(8)], axis=1).astype(BF16)
    w_qi = seg(3).astype(BF16)
    n_small = IDX_DIM + IDX_HEADS + N_HEADS_B
    w_sm = jnp.concatenate([seg(4), seg(5), seg(9), jnp.zeros((D, 2 * LANES - n_small), F32)], axis=1).astype(BF16)
    w_out0 = w_out_att[0].astype(BF16)
    wr = w_in_ret[0]
    hk, hv = N_HEADS_C * DK_C, N_HEADS_C * DV_C
    w_rq = wr[:, :hk].astype(BF16)
    w_rk = wr[:, hk:2 * hk].astype(BF16)
    w_rv = wr[:, 2 * hk:2 * hk + hv].astype(BF16)
    w_rg = wr[:, 2 * hk + hv:].astype(BF16)
    w_out1 = w_out_ret[0].astype(BF16)
    w_up = w_ffn_up.astype(BF16)
    w_down = w_ffn_down.astype(BF16)

    xp = x_prompt.reshape(Bp * S, D)
    xs = x_sample.reshape(Bd * T, D)

    q, kf, kb, vf, vb, qi, small = _att_projections(xp.astype(BF16), w_q, w_k, w_v, w_qi, w_sm)
    ki_p = small[:, :IDX_DIM]
    kit = ki_p.reshape(Bp, S, IDX_DIM).transpose(0, 2, 1).astype(BF16)
    oa = dsa_prompt(qi, small, kit, q, kb, vb, batch=Bp, s_len=S)
    fbt = small[:, IDX_DIM + IDX_HEADS:n_small].reshape(Bp, S, N_HEADS_B).transpose(0, 2, 1)
    lft_p, c_p = fox_prep(fbt, b_forget[0])
    ob = fox_prompt(c_p, q, kb, vb, batch=Bp, s_len=S)
    (yp,) = matmul(jnp.concatenate([oa, ob], axis=1), w_out0, [F32], **_mm_blocks(Bp * S))
    xp_f, xp_b = ln_residual(xp, yp, ln_mix_g[0], ln_mix_b[0])
    xp_f, xp_b = _ffn(xp_f, xp_b, w_up[0], w_down[0], ln_ffn_g[0], ln_ffn_b[0])

    new_k_p = kf.reshape(1, Bp, S, wab // HEAD_DIM, HEAD_DIM)
    new_v_p = vf.reshape(1, Bp, S, wab // HEAD_DIM, HEAD_DIM)
    new_ik_p = ki_p.reshape(1, Bp, S, IDX_DIM)
    new_lf_p = lft_p.transpose(0, 2, 1).reshape(1, Bp, S, N_HEADS_B)

    qs, kfs, kbs, vfs, vbs, qis, small_s = _att_projections(xs.astype(BF16), w_q, w_k, w_v, w_qi, w_sm)
    ki_s = small_s[:, :IDX_DIM].reshape(Bd, T, IDX_DIM)
    wi_s = small_s[:, IDX_DIM:IDX_DIM + IDX_HEADS].reshape(Bd, T, IDX_HEADS)
    fbt_s = small_s[:, IDX_DIM + IDX_HEADS:n_small].reshape(Bd, T, N_HEADS_B).transpose(0, 2, 1)
    lft_s, _ = fox_prep(jnp.pad(fbt_s, ((0, 0), (0, 0), (0, LANES - T))), b_forget[0])
    lft_s = lft_s[:, :, :T]
    pad_rows = lambda z: jnp.pad(z, ((0, 0), (0, PAGE_SIZE - T), (0, 0)))
    qi_s = qis.reshape(IDX_HEADS, Bd, T, IDX_DIM).transpose(1, 0, 2, 3).reshape(Bd, IDX_HEADS * T, IDX_DIM)
    wsb = jnp.broadcast_to((wi_s * IDX_SCALE).transpose(0, 2, 1).reshape(Bd, IDX_HEADS * T, 1),
                           (Bd, IDX_HEADS * T, LANES))
    sc = samp_scores(page_table, cache_idx_k[0], pad_rows(ki_s), qi_s, wsb)
    thr = samp_select(sc, min(TOPK_MAX, (P + T) // 4))
    eye = jnp.eye(N_HEADS_A, dtype=BF16)
    q4 = qs.reshape(Bd, T, 2, N_HEADS_A, HEAD_DIM)
    qbd = lambda g: jnp.einsum("bthd,hg->bhtgd", q4[:, :, g], eye).reshape(Bd, N_HEADS_A * T, wa)
    lfnt_pad = jnp.pad(lft_s, ((0, 0), (0, 0), (0, PAGE_SIZE - T)))
    oa_s, ob_s = samp_attend(page_table, cache_k[0].reshape(n_pool, PAGE_SIZE, wab),
                             cache_v[0].reshape(n_pool, PAGE_SIZE, wab),
                             cache_logf[0].transpose(0, 2, 1), sc, thr, qbd(0), qbd(1),
                             pad_rows(kfs.reshape(Bd, T, wab)), pad_rows(vfs.reshape(Bd, T, wab)), lfnt_pad)
    unrow = lambda o: o.reshape(Bd, N_HEADS_A, T, HEAD_DIM).transpose(0, 2, 1, 3).reshape(Bd * T, wa)
    o_s = jnp.concatenate([unrow(oa_s), unrow(ob_s)], axis=1).astype(BF16)
    (ys,) = matmul(o_s, w_out0, [F32], **_mm_blocks(Bd * T))
    xs_f, xs_b = ln_residual(xs, ys, ln_mix_g[0], ln_mix_b[0])
    xs_f, xs_b = _ffn(xs_f, xs_b, w_up[0], w_down[0], ln_ffn_g[0], ln_ffn_b[0])

    new_k_s = kfs.reshape(1, Bd, T, wab // HEAD_DIM, HEAD_DIM)
    new_v_s = vfs.reshape(1, Bd, T, wab // HEAD_DIM, HEAD_DIM)
    new_ik_s = ki_s.reshape(1, Bd, T, IDX_DIM)
    new_lf_s = lft_s.transpose(0, 2, 1).reshape(1, Bd, T, N_HEADS_B)

    lg = jnp.log1p(-jnp.exp2(-5.0 - jnp.arange(N_HEADS_C, dtype=F32)))
    lg_tab = jnp.broadcast_to(lg[:, None, None], (N_HEADS_C, 1, LANES))
    gn = gn_ret[0].reshape(N_HEADS_C, 1, DV_C)

    def ret_layer(x_f, x_b, pos0, s0, batch, t_len):
        blk = _mm_blocks(x_b.shape[0])
        (zq,) = matmul(x_b, w_rq, [F32], **blk)
        (zk,) = matmul(x_b, w_rk, [F32], **blk)
        act_dtype = BF16 if t_len % 16 == 0 else F32
        (zv,) = matmul(x_b, w_rv, [act_dtype], **blk)
        (zg,) = matmul(x_b, w_rg, [F32], **blk)
        cos, sin = rope_tables(pos0, t_len)
        gated, s_new = retention(zq, zk, zv, zg, cos, sin, lg_tab, gn, s0, batch=batch, t_len=t_len,
                                 out_dtype=act_dtype)
        (y,) = matmul(gated.astype(BF16), w_out1, [F32], bk=4096, **blk)
        x_f, x_b = ln_residual(x_f, y, ln_mix_g[1], ln_mix_b[1])
        x_f, _ = _ffn(x_f, x_b, w_up[1], w_down[1], ln_ffn_g[1], ln_ffn_b[1])
        return x_f, s_new

    yp_out, sp = ret_layer(xp_f, xp_b, 0, None, Bp, S)
    ys_out, ss = ret_layer(xs_f, xs_b, P, state_ret[0], Bd, T)

    return (yp_out.reshape(Bp, S, D), ys_out.reshape(Bd, T, D),
            new_k_p, new_v_p, new_ik_p, new_lf_p, sp[None],
            new_k_s, new_v_s, new_ik_s, new_lf_s, ss[None])
```

```python
import functools
import math

import jax
import jax.numpy as jnp
import numpy as np
from jax import lax
from jax.experimental import pallas as pl
from jax.experimental.pallas import tpu as pltpu

F32 = jnp.float32
BF16 = jnp.bfloat16

HEAD_DIM = 128
N_HEADS_A = 16
N_HEADS_B = 16
IDX_HEADS = 64
IDX_DIM = 128
IDX_SCALE = (IDX_HEADS * IDX_DIM) ** -0.5
TOPK_MAX = 256
PAGE_SIZE = 128
DK_C = 256
DV_C = 512
N_HEADS_C = 16
RET_CHUNK = 128
ROPE_BASE = 10000.0
DEPTH = 2
ALPHA = (2 * DEPTH) ** 0.25
LN_EPS = 1e-5
ATT_SCALE = HEAD_DIM ** -0.5

LANES = 128
SUBLANES = 8
VMEM_LIMIT_BYTES = 56 * 1024 * 1024
NEG_BIG = -1e30
INT_MIN = -2 ** 31
CAUSAL_SPANS = 4
MM_BLOCK = 1024
SCORE_PAGES_PER_STEP = 8


def _cparams(sem):
    return pltpu.CompilerParams(dimension_semantics=sem, vmem_limit_bytes=VMEM_LIMIT_BYTES)


def _mm_kernel(x_ref, w_ref, *refs, nk, act, n_out):
    outs = refs[:n_out]

    def finish(acc):
        if act == "relu2":
            r = jnp.maximum(acc, 0.0)
            acc = r * r
        for o in outs:
            if len(o.shape) == 3:
                for hh in range(o.shape[0]):
                    o[hh] = acc[:, hh * LANES:(hh + 1) * LANES].astype(o.dtype)
            else:
                o[...] = acc.astype(o.dtype)

    if nk == 1:
        finish(jnp.dot(x_ref[...], w_ref[...], preferred_element_type=F32))
    else:
        acc_ref = refs[n_out]
        k = pl.program_id(2)

        @pl.when(k == 0)
        def _():
            acc_ref[...] = jnp.zeros_like(acc_ref)

        acc_ref[...] += jnp.dot(x_ref[...], w_ref[...], preferred_element_type=F32)

        @pl.when(k == nk - 1)
        def _():
            finish(acc_ref[...])


def matmul(x, w, out_dtypes, *, bm, bn, bk=None, act=None, head_major=False):
    M, K = x.shape
    K2, N = w.shape
    assert K == K2
    bk = bk or K
    bm = min(bm, M)
    bn = min(bn, N)
    assert M % bm == 0 and N % bn == 0 and K % bk == 0
    nk = K // bk
    grid = (N // bn, M // bm, nk)
    in_specs = [pl.BlockSpec((bm, bk), lambda j, i, k: (i, k)),
                pl.BlockSpec((bk, bn), lambda j, i, k: (k, j))]
    if head_major:
        out_specs = [pl.BlockSpec((bn // LANES, bm, LANES), lambda j, i, k: (j, i, 0)) for _ in out_dtypes]
        out_shape = [jax.ShapeDtypeStruct((N // LANES, M, LANES), d) for d in out_dtypes]
    else:
        out_specs = [pl.BlockSpec((bm, bn), lambda j, i, k: (i, j)) for _ in out_dtypes]
        out_shape = [jax.ShapeDtypeStruct((M, N), d) for d in out_dtypes]
    scratch = [pltpu.VMEM((bm, bn), F32)] if nk > 1 else []
    res = pl.pallas_call(
        functools.partial(_mm_kernel, nk=nk, act=act, n_out=len(out_dtypes)),
        grid=grid, in_specs=in_specs, out_specs=out_specs, out_shape=out_shape,
        scratch_shapes=scratch,
        compiler_params=_cparams(("parallel", "parallel", "arbitrary")),
        name="matmul",
    )(x, w)
    return res


def _ln_kernel(x_ref, y_ref, g_ref, b_ref, of_ref, ob_ref):
    z = ALPHA * x_ref[...] + y_ref[...]
    mu = jnp.mean(z, axis=-1, keepdims=True)
    zc = z - mu
    var = jnp.mean(zc * zc, axis=-1, keepdims=True)
    r = zc * lax.rsqrt(var + LN_EPS) * g_ref[...] + b_ref[...]
    of_ref[...] = r
    ob_ref[...] = r.astype(BF16)


def ln_residual(x, y, g, b, *, bm=128):
    M, D = x.shape
    bm = min(bm, M)
    row = pl.BlockSpec((bm, D), lambda i: (i, 0))
    vec = pl.BlockSpec((1, D), lambda i: (0, 0))
    return pl.pallas_call(
        _ln_kernel, grid=(M // bm,), in_specs=[row, row, vec, vec], out_specs=[row, row],
        out_shape=[jax.ShapeDtypeStruct((M, D), F32), jax.ShapeDtypeStruct((M, D), BF16)],
        compiler_params=_cparams(("parallel",)), name="ln_residual",
    )(x, y, g.reshape(1, D), b.reshape(1, D))


def _monotone_key(x):
    bits = lax.bitcast_convert_type(x, jnp.int32)
    return jnp.where(bits < 0, bits ^ jnp.int32(0x7FFFFFFF), bits)


def _kth_largest_key(load_keys, rows, topk):
    def radix_body(it, u):
        cand_u = u | jnp.left_shift(jnp.int32(1), 31 - it)
        cand = cand_u ^ jnp.int32(INT_MIN)
        cnt = jnp.sum(jnp.where(load_keys() >= cand, 1.0, 0.0), axis=-1, keepdims=True)
        return jnp.where(cnt >= float(topk), cand_u, u)

    u = lax.fori_loop(0, 32, radix_body, jnp.zeros((rows, 1), jnp.int32))
    return u ^ jnp.int32(INT_MIN)


def _attend_heads(q_ref, k_ref, v_ref, o_ref, bias_fn, n_heads, n_keys):
    for h in range(n_heads):
        cs = slice(h * HEAD_DIM, (h + 1) * HEAD_DIM)
        s = lax.dot_general(q_ref[:, cs], k_ref[:n_keys, cs], (((1,), (1,)), ((), ())),
                            preferred_element_type=F32) * ATT_SCALE + bias_fn(h)
        m = jnp.max(s, axis=-1, keepdims=True)
        p = jnp.exp(s - m)
        l = jnp.sum(p, axis=-1, keepdims=True)
        o = jnp.dot(p.astype(BF16), v_ref[:n_keys, cs], preferred_element_type=F32) / l
        o_ref[:, cs] = o.astype(o_ref.dtype)


def _causal_span_dispatch(i, nq, s_len, body):
    spans = min(CAUSAL_SPANS, nq)
    per = nq // spans
    for j in range(spans):
        @pl.when(i // per == j)
        def _(j=j):
            body((j + 1) * (s_len // spans))


def _dsa_prompt_kernel(qi_ref, wi_ref, kit_ref, q_ref, k_ref, v_ref, o_ref, sc_ref, key_ref, wb_ref,
                       *, tq, s_len, nq, topk):
    i = pl.program_id(1)
    w = wi_ref[:, :IDX_HEADS] * IDX_SCALE
    for h in range(IDX_HEADS):
        wb_ref[h] = jnp.broadcast_to(w[:, h:h + 1], (tq, LANES))
    qpos = i * tq + lax.broadcasted_iota(jnp.int32, (tq, 1), 0)

    def body(n_keys):
        kpos = lax.broadcasted_iota(jnp.int32, (1, n_keys), 1)
        causal = kpos <= qpos
        sc_ref[:, :n_keys] = jnp.zeros((tq, n_keys), F32)

        def head_body(h, carry):
            s = jnp.dot(qi_ref[h], kit_ref[0, :, :n_keys], preferred_element_type=F32)
            wfull = jnp.concatenate([wb_ref[h]] * (n_keys // LANES), axis=1)
            sc_ref[:, :n_keys] += jnp.maximum(s, 0.0) * wfull
            return carry

        lax.fori_loop(0, IDX_HEADS, head_body, 0, unroll=2)

        key_ref[:, :n_keys] = _monotone_key(jnp.where(causal, sc_ref[:, :n_keys], -jnp.inf))
        thr = _kth_largest_key(lambda: key_ref[:, :n_keys], tq, topk)
        sel = jnp.logical_and(key_ref[:, :n_keys] >= thr, causal)
        sc_ref[:, :n_keys] = jnp.where(sel, 0.0, NEG_BIG)
        _attend_heads(q_ref, k_ref, v_ref, o_ref, lambda h: sc_ref[:, :n_keys], N_HEADS_A, n_keys)

    _causal_span_dispatch(i, nq, s_len, body)


def dsa_prompt(qi_hm, small, kit, q, k, v, *, batch, s_len, tq=128):
    nq = s_len // tq
    wa = N_HEADS_A * HEAD_DIM
    topk = min(TOPK_MAX, s_len // 4)
    kv_spec = pl.BlockSpec((s_len, wa), lambda b, i: (b, 0), pipeline_mode=pl.Buffered(1))
    return pl.pallas_call(
        functools.partial(_dsa_prompt_kernel, tq=tq, s_len=s_len, nq=nq, topk=topk),
        grid=(batch, nq),
        in_specs=[pl.BlockSpec((IDX_HEADS, tq, IDX_DIM), lambda b, i: (0, b * nq + i, 0)),
                  pl.BlockSpec((tq, LANES), lambda b, i: (b * nq + i, 1)),
                  pl.BlockSpec((1, IDX_DIM, s_len), lambda b, i: (b, 0, 0)),
                  pl.BlockSpec((tq, wa), lambda b, i: (b * nq + i, 0)),
                  kv_spec, kv_spec],
        out_specs=pl.BlockSpec((tq, wa), lambda b, i: (b * nq + i, 0)),
        out_shape=jax.ShapeDtypeStruct((batch * s_len, wa), BF16),
        scratch_shapes=[pltpu.VMEM((tq, s_len), F32), pltpu.VMEM((tq, s_len), jnp.int32),
                        pltpu.VMEM((IDX_HEADS, tq, LANES), F32)],
        compiler_params=_cparams(("parallel", "arbitrary")), name="dsa_prompt",
    )(qi_hm, small, kit, q, k, v)


def _cumsum_lanes(x):
    n = x.shape[-1]
    lane = lax.broadcasted_iota(jnp.int32, x.shape, len(x.shape) - 1)
    sh = 1
    while sh < n:
        x = x + jnp.where(lane >= sh, pltpu.roll(x, sh, len(x.shape) - 1), 0.0)
        sh *= 2
    return x


def _fox_prep_kernel(fb_ref, bf_ref, lf_ref, c_ref):
    logf = jax.nn.log_sigmoid(fb_ref[0] + bf_ref[...])
    lf_ref[0] = logf
    c_ref[0] = _cumsum_lanes(logf)


def fox_prep(fbt, b_f):
    B, H, T = fbt.shape
    blk = pl.BlockSpec((1, H, T), lambda b: (b, 0, 0))
    return pl.pallas_call(
        _fox_prep_kernel, grid=(B,),
        in_specs=[blk, pl.BlockSpec((H, 1), lambda b: (0, 0))],
        out_specs=[blk, blk],
        out_shape=[jax.ShapeDtypeStruct((B, H, T), F32)] * 2,
        compiler_params=_cparams(("parallel",)), name="fox_prep",
    )(fbt, b_f.reshape(H, 1))


def _fox_prompt_kernel(c_ref, q_ref, k_ref, v_ref, o_ref, *, tq, s_len, nq):
    i = pl.program_id(1)
    qpos = i * tq + lax.broadcasted_iota(jnp.int32, (tq, 1), 0)

    def body(n_keys):
        kpos = lax.broadcasted_iota(jnp.int32, (1, n_keys), 1)
        causal_bias = jnp.where(kpos <= qpos, 0.0, NEG_BIG)

        def bias(h):
            return causal_bias - c_ref[0, h:h + 1, :n_keys]

        _attend_heads(q_ref, k_ref, v_ref, o_ref, bias, N_HEADS_B, n_keys)

    _causal_span_dispatch(i, nq, s_len, body)


def fox_prompt(c, q, k, v, *, batch, s_len, tq=128):
    nq = s_len // tq
    wb = N_HEADS_B * HEAD_DIM
    kv_spec = pl.BlockSpec((s_len, wb), lambda b, i: (b, 1), pipeline_mode=pl.Buffered(1))
    return pl.pallas_call(
        functools.partial(_fox_prompt_kernel, tq=tq, s_len=s_len, nq=nq),
        grid=(batch, nq),
        in_specs=[pl.BlockSpec((1, N_HEADS_B, s_len), lambda b, i: (b, 0, 0)),
                  pl.BlockSpec((tq, wb), lambda b, i: (b * nq + i, 1)),
                  kv_spec, kv_spec],
        out_specs=pl.BlockSpec((tq, wb), lambda b, i: (b * nq + i, 0)),
        out_shape=jax.ShapeDtypeStruct((batch * s_len, wb), BF16),
        compiler_params=_cparams(("parallel", "arbitrary")), name="fox_prompt",
    )(c, q, k, v)


def _samp_scores_kernel(pt_ref, *refs, n_pages, pps, t_new):
    cache_refs = refs[:pps]
    kin_ref, qi_ref, wsb_ref, out_ref = refs[pps:]
    p = pl.program_id(1)

    def scores(kp):
        s = lax.dot_general(qi_ref[0], kp.astype(BF16), (((1,), (1,)), ((), ())),
                            preferred_element_type=F32)
        r = jnp.maximum(s, 0.0) * wsb_ref[0]
        acc = r[0:t_new]
        for h in range(1, IDX_HEADS):
            acc = acc + r[h * t_new:(h + 1) * t_new]
        return acc

    @pl.when(p < n_pages // pps)
    def _():
        for u in range(pps):
            out_ref[0, :, u * PAGE_SIZE:(u + 1) * PAGE_SIZE] = scores(cache_refs[u][0])

    @pl.when(p == n_pages // pps)
    def _():
        sc = scores(kin_ref[0])
        t = lax.broadcasted_iota(jnp.int32, sc.shape, 0)
        s_new = lax.broadcasted_iota(jnp.int32, sc.shape, 1)
        out_ref[0] = jnp.full(out_ref.shape[1:], -jnp.inf, F32)
        out_ref[0, :, :PAGE_SIZE] = jnp.where(s_new <= t, sc, -jnp.inf)


def samp_scores(page_table, cache_ik, ki_new_pad, qi_s, wsb):
    Bd, n_pages = page_table.shape
    rows = qi_s.shape[1]
    t_new = rows // IDX_HEADS
    pps = math.gcd(n_pages, SCORE_PAGES_PER_STEP)
    n_steps = n_pages // pps

    def page_spec(u):
        return pl.BlockSpec((1, PAGE_SIZE, IDX_DIM),
                            lambda b, p, pt: (pt[b, jnp.minimum(p, n_steps - 1) * pps + u], 0, 0))

    grid_spec = pltpu.PrefetchScalarGridSpec(
        num_scalar_prefetch=1, grid=(Bd, n_steps + 1),
        in_specs=[page_spec(u) for u in range(pps)] + [
            pl.BlockSpec((1, PAGE_SIZE, IDX_DIM), lambda b, p, pt: (b, 0, 0)),
            pl.BlockSpec((1, rows, IDX_DIM), lambda b, p, pt: (b, 0, 0)),
            pl.BlockSpec((1, rows, LANES), lambda b, p, pt: (b, 0, 0))],
        out_specs=pl.BlockSpec((1, t_new, pps * PAGE_SIZE), lambda b, p, pt: (b, 0, p)))
    return pl.pallas_call(
        functools.partial(_samp_scores_kernel, n_pages=n_pages, pps=pps, t_new=t_new),
        grid_spec=grid_spec,
        out_shape=jax.ShapeDtypeStruct((Bd, t_new, (n_pages + pps) * PAGE_SIZE), F32),
        compiler_params=_cparams(("parallel", "arbitrary")), name="samp_scores",
    )(page_table, *([cache_ik] * pps), ki_new_pad, qi_s, wsb)


def _samp_select_kernel(sc_ref, thr_ref, key_ref, *, topk):
    key_ref[...] = _monotone_key(sc_ref[0])
    rows = key_ref.shape[0]
    thr = _kth_largest_key(lambda: key_ref[...], rows, topk)
    thr_ref[0] = jnp.broadcast_to(thr, (rows, LANES))


def samp_select(sc, topk):
    Bd, T, L = sc.shape
    return pl.pallas_call(
        functools.partial(_samp_select_kernel, topk=topk), grid=(Bd,),
        in_specs=[pl.BlockSpec((1, T, L), lambda b: (b, 0, 0))],
        out_specs=pl.BlockSpec((1, T, LANES), lambda b: (b, 0, 0)),
        out_shape=jax.ShapeDtypeStruct((Bd, T, LANES), jnp.int32),
        scratch_shapes=[pltpu.VMEM((T, L), jnp.int32)],
        compiler_params=_cparams(("parallel",)), name="samp_select",
    )(sc)


def _samp_attend_kernel(pt_ref, ck_ref, cv_ref, clf_ref, sc_ref, thr_ref, q_ref,
                        kn_ref, vn_ref, lfn_ref, oa_ref, ob_ref, m_ref, l_ref, acc_ref, carry_ref,
                        *, n_pages, t_new):
    p = pl.program_id(1)
    n_rows = N_HEADS_A * t_new
    n_heads = N_HEADS_A + N_HEADS_B

    @pl.when(p == 0)
    def _():
        m_ref[...] = jnp.full_like(m_ref, NEG_BIG)
        l_ref[...] = jnp.zeros_like(l_ref)
        acc_ref[...] = jnp.zeros_like(acc_ref)
        carry_ref[...] = jnp.zeros_like(carry_ref)

    def expand_heads(x):
        return jnp.concatenate(
            [jnp.broadcast_to(x[h:h + 1, :], (t_new, x.shape[1])) for h in range(N_HEADS_B)], axis=0)

    def head_rows(ref, head):
        return ref[0, pl.ds(head, PAGE_SIZE, stride=n_heads), :].astype(BF16)

    def update(g, k_ref, v_ref, logits_fn, mask):
        h0 = g * N_HEADS_A
        s = jnp.concatenate(
            [lax.dot_general(q_ref[0, h0 + h].astype(BF16), head_rows(k_ref, h0 + h),
                             (((1,), (1,)), ((), ())), preferred_element_type=F32)
             for h in range(N_HEADS_A)], axis=0) * ATT_SCALE
        s = jnp.where(mask, logits_fn(s), NEG_BIG)
        m_old = m_ref[g]
        m_new = jnp.maximum(m_old, jnp.max(s, axis=-1, keepdims=True))
        pe = jnp.where(mask, jnp.exp(s - m_new), 0.0)
        alpha = jnp.exp(m_old - m_new)
        l_ref[g] = alpha * l_ref[g] + jnp.sum(pe, axis=-1, keepdims=True)
        pv = jnp.concatenate(
            [jnp.dot(pe[h * t_new:(h + 1) * t_new].astype(BF16), head_rows(v_ref, h0 + h),
                     preferred_element_type=F32) for h in range(N_HEADS_A)], axis=0)
        acc_ref[g] = alpha * acc_ref[g] + pv
        m_ref[g] = m_new

    def sel_mask():
        sel = jnp.where(_monotone_key(sc_ref[0]) >= thr_ref[0], 1.0, 0.0)
        return jnp.concatenate([sel] * N_HEADS_A, axis=0) > 0.5

    @pl.when(p == 0)
    def _():
        row_t = lax.broadcasted_iota(jnp.int32, (n_rows, PAGE_SIZE), 0) % t_new
        s_new = lax.broadcasted_iota(jnp.int32, (n_rows, PAGE_SIZE), 1)
        causal = s_new <= row_t
        update(0, kn_ref, vn_ref, lambda s: s, jnp.logical_and(sel_mask(), causal))
        cn = expand_heads(_cumsum_lanes(lfn_ref[0]))
        update(1, kn_ref, vn_ref, lambda s: s - cn, causal)

    @pl.when(p > 0)
    def _():
        update(0, ck_ref, cv_ref, lambda s: s, sel_mask())
        pre = _cumsum_lanes(clf_ref[0])
        tot = pre[:, PAGE_SIZE - 1:PAGE_SIZE]
        r = expand_heads(tot - pre) + carry_ref[...]
        carry_ref[...] += expand_heads(tot)
        update(1, ck_ref, cv_ref, lambda s: s + r, jnp.full((n_rows, PAGE_SIZE), True))

    @pl.when(p == n_pages)
    def _():
        oa_ref[0] = acc_ref[0] / l_ref[0]
        ob_ref[0] = acc_ref[1] / l_ref[1]


def samp_attend(page_table, cache_k, cache_v, cache_lft, sc, thr, q_hm, kn_pad, vn_pad, lfnt_pad):
    Bd, n_pages = page_table.shape
    n_heads, t_new = q_hm.shape[1], q_hm.shape[2]
    n_rows = N_HEADS_A * t_new
    page_rows = cache_k.shape[1]

    def page(b, p, pt):
        return pt[b, n_pages - jnp.maximum(p, 1)]

    def sc_col(b, p, pt):
        return jnp.where(p == 0, n_pages, n_pages - jnp.maximum(p, 1))

    grid_spec = pltpu.PrefetchScalarGridSpec(
        num_scalar_prefetch=1, grid=(Bd, n_pages + 1),
        in_specs=[pl.BlockSpec((1, page_rows, HEAD_DIM), lambda b, p, pt: (page(b, p, pt), 0, 0)),
                  pl.BlockSpec((1, page_rows, HEAD_DIM), lambda b, p, pt: (page(b, p, pt), 0, 0)),
                  pl.BlockSpec((1, N_HEADS_B, PAGE_SIZE), lambda b, p, pt: (page(b, p, pt), 0, 0)),
                  pl.BlockSpec((1, t_new, PAGE_SIZE), lambda b, p, pt: (b, 0, sc_col(b, p, pt))),
                  pl.BlockSpec((1, t_new, LANES), lambda b, p, pt: (b, 0, 0)),
                  pl.BlockSpec((1, n_heads, t_new, HEAD_DIM), lambda b, p, pt: (b, 0, 0, 0)),
                  pl.BlockSpec((1, page_rows, HEAD_DIM), lambda b, p, pt: (b, 0, 0)),
                  pl.BlockSpec((1, page_rows, HEAD_DIM), lambda b, p, pt: (b, 0, 0)),
                  pl.BlockSpec((1, N_HEADS_B, PAGE_SIZE), lambda b, p, pt: (b, 0, 0))],
        out_specs=[pl.BlockSpec((1, n_rows, HEAD_DIM), lambda b, p, pt: (b, 0, 0))] * 2,
        scratch_shapes=[pltpu.VMEM((2, n_rows, 1), F32), pltpu.VMEM((2, n_rows, 1), F32),
                        pltpu.VMEM((2, n_rows, HEAD_DIM), F32), pltpu.VMEM((n_rows, 1), F32)])
    return pl.pallas_call(
        functools.partial(_samp_attend_kernel, n_pages=n_pages, t_new=t_new),
        grid_spec=grid_spec,
        out_shape=[jax.ShapeDtypeStruct((Bd, n_rows, HEAD_DIM), F32)] * 2,
        compiler_params=_cparams(("parallel", "arbitrary")), name="samp_attend",
    )(page_table, cache_k, cache_v, cache_lft, sc, thr, q_hm, kn_pad, vn_pad, lfnt_pad)


def _rope_kernel(inv_ref, cos_ref, sin_ref, *, pos0):
    n = cos_ref.shape[0]
    pos = (pos0 + lax.broadcasted_iota(jnp.int32, (n, LANES), 0)).astype(F32)
    ang = pos * inv_ref[...]
    cos_ref[...] = jnp.cos(ang)
    sin_ref[...] = jnp.sin(ang)


def rope_tables(pos0, n):
    half = DK_C // 2
    inv = (ROPE_BASE ** (-jnp.arange(half, dtype=F32) / half)).reshape(1, half)
    return pl.pallas_call(
        functools.partial(_rope_kernel, pos0=pos0),
        out_shape=[jax.ShapeDtypeStruct((n, half), F32)] * 2, name="rope_tables",
    )(inv)


def _retention_kernel(*refs, chunk, n_chunks, has_s0):
    if has_s0:
        q_ref, k_ref, v_ref, g_ref, cos_ref, sin_ref, lg_ref, gn_ref, s0_ref, o_ref, sout_ref, st_ref = refs
        st_ref[...] = s0_ref[0, 0]
    else:
        q_ref, k_ref, v_ref, g_ref, cos_ref, sin_ref, lg_ref, gn_ref, o_ref, sout_ref, st_ref = refs
        st_ref[...] = jnp.zeros_like(st_ref)
    half = DK_C // 2
    lg = lg_ref[0][:, :1]
    i_col = lax.broadcasted_iota(jnp.int32, (chunk, 1), 0).astype(F32)
    j_row = lax.broadcasted_iota(jnp.int32, (1, chunk), 1).astype(F32)
    diff = i_col - j_row
    d_intra = jnp.where(diff >= 0, jnp.exp(lg * jnp.maximum(diff, 0.0)), 0.0)
    q_dec = jnp.exp(lg * (i_col + 1.0))
    k_dec = jnp.exp(lg * (chunk - 1.0 - i_col))
    c_dec = jnp.exp(lg * chunk)
    gn = gn_ref[0]

    def rot(x, cos, sin):
        x1, x2 = x[:, :half], x[:, half:]
        return jnp.concatenate([x1 * cos - x2 * sin, x1 * sin + x2 * cos], axis=1)

    def chunk_body(c, carry):
        rows = pl.ds(pl.multiple_of(c * chunk, chunk), chunk)
        cos, sin = cos_ref[rows, :], sin_ref[rows, :]
        qr = rot(q_ref[rows, :], cos, sin) * (DK_C ** -0.5)
        kr = rot(k_ref[rows, :], cos, sin)
        qb = qr.astype(BF16)
        vb = v_ref[rows, :].astype(BF16)
        a = lax.dot_general(qb, kr.astype(BF16), (((1,), (1,)), ((), ())),
                            preferred_element_type=F32) * d_intra
        st = st_ref[...]
        o = (jnp.dot(a.astype(BF16), vb, preferred_element_type=F32)
             + jnp.dot(qb, st.astype(BF16), preferred_element_type=F32) * q_dec)
        kd = (kr * k_dec).astype(BF16)
        st_ref[...] = st * c_dec + lax.dot_general(kd, vb, (((0,), (0,)), ((), ())),
                                                   preferred_element_type=F32)
        mu = jnp.mean(o, axis=-1, keepdims=True)
        oc = o - mu
        var = jnp.mean(oc * oc, axis=-1, keepdims=True)
        on = oc * lax.rsqrt(var + LN_EPS) * gn
        g = g_ref[rows, :]
        o_ref[rows, :] = (g * jax.nn.sigmoid(g) * on).astype(o_ref.dtype)
        return carry

    if n_chunks == 1:
        chunk_body(0, 0)
    else:
        lax.fori_loop(0, n_chunks, chunk_body, 0)
    sout_ref[0, 0] = st_ref[...]


def retention(zq, zk, zv, zg, cos, sin, lg_tab, gn, s0, *, batch, t_len, out_dtype):
    chunk = math.gcd(t_len, RET_CHUNK)
    n_chunks = t_len // chunk
    H = N_HEADS_C
    in_specs = [pl.BlockSpec((t_len, DK_C), lambda b, h: (b, h)),
                pl.BlockSpec((t_len, DK_C), lambda b, h: (b, h)),
                pl.BlockSpec((t_len, DV_C), lambda b, h: (b, h)),
                pl.BlockSpec((t_len, DV_C), lambda b, h: (b, h)),
                pl.BlockSpec((t_len, DK_C // 2), lambda b, h: (0, 0)),
                pl.BlockSpec((t_len, DK_C // 2), lambda b, h: (0, 0)),
                pl.BlockSpec((1, 1, LANES), lambda b, h: (h, 0, 0)),
                pl.BlockSpec((1, 1, DV_C), lambda b, h: (h, 0, 0))]
    args = [zq, zk, zv, zg, cos, sin, lg_tab, gn]
    if s0 is not None:
        in_specs.append(pl.BlockSpec((1, 1, DK_C, DV_C), lambda b, h: (b, h, 0, 0)))
        args.append(s0)
    return pl.pallas_call(
        functools.partial(_retention_kernel, chunk=chunk, n_chunks=n_chunks, has_s0=s0 is not None),
        grid=(batch, H), in_specs=in_specs,
        out_specs=[pl.BlockSpec((t_len, DV_C), lambda b, h: (b, h)),
                   pl.BlockSpec((1, 1, DK_C, DV_C), lambda b, h: (b, h, 0, 0))],
        out_shape=[jax.ShapeDtypeStruct((batch * t_len, H * DV_C), out_dtype),
                   jax.ShapeDtypeStruct((batch, H, DK_C, DV_C), F32)],
        scratch_shapes=[pltpu.VMEM((DK_C, DV_C), F32)],
        compiler_params=_cparams(("parallel", "parallel")), name="retention",
    )(*args)


def _mm_blocks(m):
    return dict(bm=min(m, MM_BLOCK), bn=MM_BLOCK)


def _ffn(x_f32, x_bf, w_up, w_down, g, b):
    m = x_bf.shape[0]
    (h,) = matmul(x_bf, w_up, [BF16], act="relu2", **_mm_blocks(m))
    (y,) = matmul(h, w_down, [F32], bk=4096, **_mm_blocks(m))
    return ln_residual(x_f32, y, g, b)


def _att_projections(x_bf, w_q, w_k, w_v, w_qi, w_sm):
    m = x_bf.shape[0]
    blk = _mm_blocks(m)
    (q,) = matmul(x_bf, w_q, [BF16], **blk)
    kf, kb = matmul(x_bf, w_k, [F32, BF16], **blk)
    vf, vb = matmul(x_bf, w_v, [F32, BF16], **blk)
    (qi,) = matmul(x_bf, w_qi, [BF16], head_major=True, **blk)
    (small,) = matmul(x_bf, w_sm, [F32], **blk)
    return q, kf, kb, vf, vb, qi, small


def kernel(x_prompt, x_sample, cache_k, cache_v, cache_idx_k, cache_logf, state_ret, page_table, w_in_att, b_forget, w_out_att, w_in_ret, gn_ret, w_out_ret, ln_mix_g, ln_mix_b, ln_ffn_g, ln_ffn_b, w_ffn_up, w_ffn_down):
    Bp, S, D = x_prompt.shape
    Bd, T, _ = x_sample.shape
    n_pages = page_table.shape[1]
    P = n_pages * PAGE_SIZE
    n_pool = cache_k.shape[1]
    wa = N_HEADS_A * HEAD_DIM
    wab = wa + N_HEADS_B * HEAD_DIM
    n_heads = wab // HEAD_DIM

    split_att = (wa, wa, wa, IDX_HEADS * IDX_DIM, IDX_DIM, IDX_HEADS, wa, wa, wa, N_HEADS_B)
    off = np.concatenate([[0], np.cumsum(split_att)])
    w0 = w_in_att[0]
    seg = lambda i: w0[:, off[i]:off[i + 1]]
    w_q = jnp.concatenate([seg(0), seg(6)], axis=1).astype(BF16)
    w_k = jnp.concatenate([seg(1), seg(7)], axis=1).astype(BF16)
    w_v = jnp.concatenate([seg(2), seg(8)], axis=1).astype(BF16)
    w_qi = seg(3).astype(BF16)
    n_small = IDX_DIM + IDX_HEADS + N_HEADS_B
    w_sm = jnp.concatenate([seg(4), seg(5), seg(9), jnp.zeros((D, 2 * LANES - n_small), F32)], axis=1).astype(BF16)
    w_out0 = w_out_att[0].astype(BF16)
    wr = w_in_ret[0]
    hk, hv = N_HEADS_C * DK_C, N_HEADS_C * DV_C
    w_rq = wr[:, :hk].astype(BF16)
    w_rk = wr[:, hk:2 * hk].astype(BF16)
    w_rv = wr[:, 2 * hk:2 * hk + hv].astype(BF16)
    w_rg = wr[:, 2 * hk + hv:].astype(BF16)
    w_out1 = w_out_ret[0].astype(BF16)
    w_up = [w_ffn_up[l].astype(BF16) for l in range(DEPTH)]
    w_down = [w_ffn_down[l].astype(BF16) for l in range(DEPTH)]

    xp = x_prompt.reshape(Bp * S, D)
    xs = x_sample.reshape(Bd * T, D)

    q, kf, kb, vf, vb, qi, small = _att_projections(xp.astype(BF16), w_q, w_k, w_v, w_qi, w_sm)
    ki_p = small[:, :IDX_DIM]
    kit = ki_p.reshape(Bp, S, IDX_DIM).transpose(0, 2, 1).astype(BF16)
    oa = dsa_prompt(qi, small, kit, q, kb, vb, batch=Bp, s_len=S)
    fbt = small[:, IDX_DIM + IDX_HEADS:n_small].reshape(Bp, S, N_HEADS_B).transpose(0, 2, 1)
    lft_p, c_p = fox_prep(fbt, b_forget[0])
    ob = fox_prompt(c_p, q, kb, vb, batch=Bp, s_len=S)
    (yp,) = matmul(jnp.concatenate([oa, ob], axis=1), w_out0, [F32], **_mm_blocks(Bp * S))
    xp_f, xp_b = ln_residual(xp, yp, ln_mix_g[0], ln_mix_b[0])
    xp_f, xp_b = _ffn(xp_f, xp_b, w_up[0], w_down[0], ln_ffn_g[0], ln_ffn_b[0])

    new_k_p = kf.reshape(1, Bp, S, n_heads, HEAD_DIM)
    new_v_p = vf.reshape(1, Bp, S, n_heads, HEAD_DIM)
    new_ik_p = ki_p.reshape(1, Bp, S, IDX_DIM)
    new_lf_p = lft_p.transpose(0, 2, 1).reshape(1, Bp, S, N_HEADS_B)

    qs, kfs, kbs, vfs, vbs, qis, small_s = _att_projections(xs.astype(BF16), w_q, w_k, w_v, w_qi, w_sm)
    ki_s = small_s[:, :IDX_DIM].reshape(Bd, T, IDX_DIM)
    wi_s = small_s[:, IDX_DIM:IDX_DIM + IDX_HEADS].reshape(Bd, T, IDX_HEADS)
    fbt_s = small_s[:, IDX_DIM + IDX_HEADS:n_small].reshape(Bd, T, N_HEADS_B).transpose(0, 2, 1)
    lft_s, _ = fox_prep(jnp.pad(fbt_s, ((0, 0), (0, 0), (0, LANES - T))), b_forget[0])
    lft_s = lft_s[:, :, :T]
    pad_rows = lambda z: jnp.pad(z, ((0, 0), (0, PAGE_SIZE - T), (0, 0)))
    qi_s = qis.reshape(IDX_HEADS, Bd, T, IDX_DIM).transpose(1, 0, 2, 3).reshape(Bd, IDX_HEADS * T, IDX_DIM)
    wsb = jnp.broadcast_to((wi_s * IDX_SCALE).transpose(0, 2, 1).reshape(Bd, IDX_HEADS * T, 1),
                           (Bd, IDX_HEADS * T, LANES))
    sc = samp_scores(page_table, cache_idx_k[0], pad_rows(ki_s), qi_s, wsb)
    thr = samp_select(sc, min(TOPK_MAX, (P + T) // 4))
    q_hm = qs.reshape(Bd, T, n_heads, HEAD_DIM).transpose(0, 2, 1, 3).astype(F32)
    lfnt_pad = jnp.pad(lft_s, ((0, 0), (0, 0), (0, PAGE_SIZE - T)))
    as_rows = lambda z, lead: z.reshape(lead, PAGE_SIZE * n_heads, HEAD_DIM)
    new_rows = lambda z: as_rows(pad_rows(z.reshape(Bd, T, wab)), Bd)
    oa_s, ob_s = samp_attend(page_table, as_rows(cache_k[0], n_pool), as_rows(cache_v[0], n_pool),
                             cache_logf[0].transpose(0, 2, 1), sc, thr, q_hm,
                             new_rows(kfs), new_rows(vfs), lfnt_pad)
    unrow = lambda o: o.reshape(Bd, N_HEADS_A, T, HEAD_DIM).transpose(0, 2, 1, 3).reshape(Bd * T, wa)
    o_s = jnp.concatenate([unrow(oa_s), unrow(ob_s)], axis=1).astype(BF16)
    (ys,) = matmul(o_s, w_out0, [F32], **_mm_blocks(Bd * T))
    xs_f, xs_b = ln_residual(xs, ys, ln_mix_g[0], ln_mix_b[0])
    xs_f, xs_b = _ffn(xs_f, xs_b, w_up[0], w_down[0], ln_ffn_g[0], ln_ffn_b[0])

    new_k_s = kfs.reshape(1, Bd, T, n_heads, HEAD_DIM)
    new_v_s = vfs.reshape(1, Bd, T, n_heads, HEAD_DIM)
    new_ik_s = ki_s.reshape(1, Bd, T, IDX_DIM)
    new_lf_s = lft_s.transpose(0, 2, 1).reshape(1, Bd, T, N_HEADS_B)

    lg = jnp.log1p(-jnp.exp2(-5.0 - jnp.arange(N_HEADS_C, dtype=F32)))
    lg_tab = jnp.broadcast_to(lg[:, None, None], (N_HEADS_C, 1, LANES))
    gn = gn_ret[0].reshape(N_HEADS_C, 1, DV_C)

    def ret_layer(x_f, x_b, pos0, s0, batch, t_len):
        blk = _mm_blocks(x_b.shape[0])
        (zq,) = matmul(x_b, w_rq, [F32], **blk)
        (zk,) = matmul(x_b, w_rk, [F32], **blk)
        act_dtype = BF16 if t_len % 16 == 0 else F32
        (zv,) = matmul(x_b, w_rv, [act_dtype], **blk)
        (zg,) = matmul(x_b, w_rg, [F32], **blk)
        cos, sin = rope_tables(pos0, t_len)
        gated, s_new = retention(zq, zk, zv, zg, cos, sin, lg_tab, gn, s0, batch=batch, t_len=t_len,
                                 out_dtype=act_dtype)
        (y,) = matmul(gated.astype(BF16), w_out1, [F32], bk=4096, **blk)
        x_f, x_b = ln_residual(x_f, y, ln_mix_g[1], ln_mix_b[1])
        x_f, _ = _ffn(x_f, x_b, w_up[1], w_down[1], ln_ffn_g[1], ln_ffn_b[1])
        return x_f, s_new

    yp_out, sp = ret_layer(xp_f, xp_b, 0, None, Bp, S)
    ys_out, ss = ret_layer(xs_f, xs_b, P, state_ret[0], Bd, T)

    return (yp_out.reshape(Bp, S, D), ys_out.reshape(Bd, T, D),
            new_k_p, new_v_p, new_ik_p, new_lf_p, sp[None],
            new_k_s, new_v_s, new_ik_s, new_lf_s, ss[None])
```

```python
import functools
import math

import jax
import jax.numpy as jnp
import numpy as np
from jax import lax
from jax.experimental import pallas as pl
from jax.experimental.pallas import tpu as pltpu

F32 = jnp.float32
BF16 = jnp.bfloat16

HEAD_DIM = 128
N_HEADS_A = 16
N_HEADS_B = 16
IDX_HEADS = 64
IDX_DIM = 128
IDX_SCALE = (IDX_HEADS * IDX_DIM) ** -0.5
TOPK_MAX = 256
PAGE_SIZE = 128
DK_C = 256
DV_C = 512
N_HEADS_C = 16
RET_CHUNK = 128
ROPE_BASE = 10000.0
DEPTH = 2
ALPHA = (2 * DEPTH) ** 0.25
LN_EPS = 1e-5
ATT_SCALE = HEAD_DIM ** -0.5

LANES = 128
SUBLANES = 8
VMEM_LIMIT_BYTES = 56 * 1024 * 1024
NEG_BIG = -1e30
INT_MIN = -2 ** 31
CAUSAL_SPANS = 4
MM_BLOCK = 1024
MM_BLOCK_CAST = 512
SCORE_PAGES_PER_STEP = 8


def _cparams(sem):
    return pltpu.CompilerParams(dimension_semantics=sem, vmem_limit_bytes=VMEM_LIMIT_BYTES)


def _mm_store(outs, acc, act):
    if act == "relu2":
        r = jnp.maximum(acc, 0.0)
        acc = r * r
    for o in outs:
        if len(o.shape) == 3:
            for hh in range(o.shape[0]):
                o[hh] = acc[:, hh * LANES:(hh + 1) * LANES].astype(o.dtype)
        else:
            o[...] = acc.astype(o.dtype)


def _mm_kernel(x_ref, xs_ref, w_ref, *refs, nk, act, n_out, cast_w):
    outs, outs_s = refs[:n_out], refs[n_out:2 * n_out]
    scratch = list(refs[2 * n_out:])
    i = pl.program_id(1)
    if cast_w:
        wbf_ref = scratch.pop()

        @pl.when(i == 0)
        def _():
            wbf_ref[...] = w_ref[...].astype(BF16)

        w_ref = wbf_ref

    if nk == 1:
        _mm_store(outs, jnp.dot(x_ref[...], w_ref[...], preferred_element_type=F32), act)

        @pl.when(i == 0)
        def _():
            _mm_store(outs_s, jnp.dot(xs_ref[...], w_ref[...], preferred_element_type=F32), act)
    else:
        acc_ref, acc_s_ref = scratch
        k = pl.program_id(2)

        @pl.when(k == 0)
        def _():
            acc_ref[...] = jnp.zeros_like(acc_ref)

        acc_ref[...] += jnp.dot(x_ref[...], w_ref[...], preferred_element_type=F32)

        @pl.when(k == nk - 1)
        def _():
            _mm_store(outs, acc_ref[...], act)

        @pl.when(i == 0)
        def _():
            @pl.when(k == 0)
            def _():
                acc_s_ref[...] = jnp.zeros_like(acc_s_ref)

            acc_s_ref[...] += jnp.dot(xs_ref[...], w_ref[...], preferred_element_type=F32)

            @pl.when(k == nk - 1)
            def _():
                _mm_store(outs_s, acc_s_ref[...], act)


def matmul(x, xs, w, out_dtypes, side_dtypes=None, *, layer=None, col0=0, n_cols=None, bk=None, act=None,
           head_major=False):
    M, K = x.shape
    Ms = xs.shape[0]
    assert w.shape[-2] == K and (layer is None) == (len(w.shape) == 2)
    N = n_cols or w.shape[-1]
    cast_w = w.dtype == F32
    side_dtypes = side_dtypes or out_dtypes
    bk = bk or K
    bm = min(MM_BLOCK, M)
    bn = min(MM_BLOCK_CAST if cast_w else MM_BLOCK, N)
    assert M % bm == 0 and N % bn == 0 and K % bk == 0 and col0 % bn == 0
    nk = K // bk
    assert not (cast_w and nk > 1)
    jb0 = col0 // bn
    if layer is None:
        w_spec = pl.BlockSpec((bk, bn), lambda j, i, k: (k, j + jb0))
    else:
        w_spec = pl.BlockSpec((None, bk, bn), lambda j, i, k: (layer, k, j + jb0))
    in_specs = [pl.BlockSpec((bm, bk), lambda j, i, k: (i, k)),
                pl.BlockSpec((Ms, bk), lambda j, i, k: (0, k)),
                w_spec]
    if head_major:
        hb = bn // LANES
        out_specs = ([pl.BlockSpec((hb, bm, LANES), lambda j, i, k: (j, i, 0)) for _ in out_dtypes]
                     + [pl.BlockSpec((hb, Ms, LANES), lambda j, i, k: (j, 0, 0)) for _ in side_dtypes])
        out_shape = ([jax.ShapeDtypeStruct((N // LANES, M, LANES), d) for d in out_dtypes]
                     + [jax.ShapeDtypeStruct((N // LANES, Ms, LANES), d) for d in side_dtypes])
    else:
        out_specs = ([pl.BlockSpec((bm, bn), lambda j, i, k: (i, j)) for _ in out_dtypes]
                     + [pl.BlockSpec((Ms, bn), lambda j, i, k: (0, j)) for _ in side_dtypes])
        out_shape = ([jax.ShapeDtypeStruct((M, N), d) for d in out_dtypes]
                     + [jax.ShapeDtypeStruct((Ms, N), d) for d in side_dtypes])
    scratch = [pltpu.VMEM((bm, bn), F32), pltpu.VMEM((Ms, bn), F32)] if nk > 1 else []
    if cast_w:
        scratch.append(pltpu.VMEM((bk, bn), BF16))
    res = pl.pallas_call(
        functools.partial(_mm_kernel, nk=nk, act=act, n_out=len(out_dtypes), cast_w=cast_w),
        grid=(N // bn, M // bm, nk), in_specs=in_specs, out_specs=out_specs, out_shape=out_shape,
        scratch_shapes=scratch,
        compiler_params=_cparams(("parallel", "arbitrary", "arbitrary")),
        name="matmul",
    )(x, xs, w)
    return res[:len(out_dtypes)], res[len(out_dtypes):]


def _ln_kernel(x_ref, y_ref, g_ref, b_ref, of_ref, ob_ref):
    z = ALPHA * x_ref[...] + y_ref[...]
    mu = jnp.mean(z, axis=-1, keepdims=True)
    zc = z - mu
    var = jnp.mean(zc * zc, axis=-1, keepdims=True)
    r = zc * lax.rsqrt(var + LN_EPS) * g_ref[...] + b_ref[...]
    of_ref[...] = r
    ob_ref[...] = r.astype(BF16)


def ln_residual(x, y, g, b, *, bm=128):
    M, D = x.shape
    bm = min(bm, M)
    row = pl.BlockSpec((bm, D), lambda i: (i, 0))
    vec = pl.BlockSpec((1, D), lambda i: (0, 0))
    return pl.pallas_call(
        _ln_kernel, grid=(M // bm,), in_specs=[row, row, vec, vec], out_specs=[row, row],
        out_shape=[jax.ShapeDtypeStruct((M, D), F32), jax.ShapeDtypeStruct((M, D), BF16)],
        compiler_params=_cparams(("parallel",)), name="ln_residual",
    )(x, y, g.reshape(1, D), b.reshape(1, D))


def _monotone_key(x):
    bits = lax.bitcast_convert_type(x, jnp.int32)
    return jnp.where(bits < 0, bits ^ jnp.int32(0x7FFFFFFF), bits)


def _kth_largest_key(load_keys, rows, topk):
    def radix_body(it, u):
        cand_u = u | jnp.left_shift(jnp.int32(1), 31 - it)
        cand = cand_u ^ jnp.int32(INT_MIN)
        cnt = jnp.sum(jnp.where(load_keys() >= cand, 1.0, 0.0), axis=-1, keepdims=True)
        return jnp.where(cnt >= float(topk), cand_u, u)

    u = lax.fori_loop(0, 32, radix_body, jnp.zeros((rows, 1), jnp.int32))
    return u ^ jnp.int32(INT_MIN)


def _attend_heads(q_ref, k_ref, v_ref, o_ref, bias_fn, n_heads, n_keys):
    for h in range(n_heads):
        cs = slice(h * HEAD_DIM, (h + 1) * HEAD_DIM)
        s = lax.dot_general(q_ref[:, cs], k_ref[:n_keys, cs], (((1,), (1,)), ((), ())),
                            preferred_element_type=F32) * ATT_SCALE + bias_fn(h)
        m = jnp.max(s, axis=-1, keepdims=True)
        p = jnp.exp(s - m)
        l = jnp.sum(p, axis=-1, keepdims=True)
        o = jnp.dot(p.astype(BF16), v_ref[:n_keys, cs], preferred_element_type=F32) / l
        o_ref[:, cs] = o.astype(o_ref.dtype)


def _causal_span_dispatch(i, nq, s_len, body):
    spans = min(CAUSAL_SPANS, nq)
    per = nq // spans
    for j in range(spans):
        @pl.when(i // per == j)
        def _(j=j):
            body((j + 1) * (s_len // spans))


def _dsa_prompt_kernel(qi_ref, wi_ref, kit_ref, q_ref, k_ref, v_ref, o_ref, sc_ref, key_ref, wb_ref,
                       *, tq, s_len, nq, topk):
    i = pl.program_id(1)
    w = wi_ref[:, :IDX_HEADS] * IDX_SCALE
    for h in range(IDX_HEADS):
        wb_ref[h] = jnp.broadcast_to(w[:, h:h + 1], (tq, LANES))
    qpos = i * tq + lax.broadcasted_iota(jnp.int32, (tq, 1), 0)

    def body(n_keys):
        kpos = lax.broadcasted_iota(jnp.int32, (1, n_keys), 1)
        causal = kpos <= qpos
        sc_ref[:, :n_keys] = jnp.zeros((tq, n_keys), F32)

        def head_body(h, carry):
            s = jnp.dot(qi_ref[h], kit_ref[0, :, :n_keys], preferred_element_type=F32)
            wfull = jnp.concatenate([wb_ref[h]] * (n_keys // LANES), axis=1)
            sc_ref[:, :n_keys] += jnp.maximum(s, 0.0) * wfull
            return carry

        lax.fori_loop(0, IDX_HEADS, head_body, 0, unroll=2)

        key_ref[:, :n_keys] = _monotone_key(jnp.where(causal, sc_ref[:, :n_keys], -jnp.inf))
        thr = _kth_largest_key(lambda: key_ref[:, :n_keys], tq, topk)
        sel = jnp.logical_and(key_ref[:, :n_keys] >= thr, causal)
        sc_ref[:, :n_keys] = jnp.where(sel, 0.0, NEG_BIG)
        _attend_heads(q_ref, k_ref, v_ref, o_ref, lambda h: sc_ref[:, :n_keys], N_HEADS_A, n_keys)

    _causal_span_dispatch(i, nq, s_len, body)


def dsa_prompt(qi_hm, small, kit, q, k, v, *, batch, s_len, tq=128):
    nq = s_len // tq
    wa = N_HEADS_A * HEAD_DIM
    topk = min(TOPK_MAX, s_len // 4)
    kv_spec = pl.BlockSpec((s_len, wa), lambda b, i: (b, 0), pipeline_mode=pl.Buffered(1))
    return pl.pallas_call(
        functools.partial(_dsa_prompt_kernel, tq=tq, s_len=s_len, nq=nq, topk=topk),
        grid=(batch, nq),
        in_specs=[pl.BlockSpec((IDX_HEADS, tq, IDX_DIM), lambda b, i: (0, b * nq + i, 0)),
                  pl.BlockSpec((tq, LANES), lambda b, i: (b * nq + i, 1)),
                  pl.BlockSpec((1, IDX_DIM, s_len), lambda b, i: (b, 0, 0)),
                  pl.BlockSpec((tq, wa), lambda b, i: (b * nq + i, 0)),
                  kv_spec, kv_spec],
        out_specs=pl.BlockSpec((tq, wa), lambda b, i: (b * nq + i, 0)),
        out_shape=jax.ShapeDtypeStruct((batch * s_len, wa), BF16),
        scratch_shapes=[pltpu.VMEM((tq, s_len), F32), pltpu.VMEM((tq, s_len), jnp.int32),
                        pltpu.VMEM((IDX_HEADS, tq, LANES), F32)],
        compiler_params=_cparams(("parallel", "arbitrary")), name="dsa_prompt",
    )(qi_hm, small, kit, q, k, v)


def _cumsum_lanes(x):
    n = x.shape[-1]
    lane = lax.broadcasted_iota(jnp.int32, x.shape, len(x.shape) - 1)
    sh = 1
    while sh < n:
        x = x + jnp.where(lane >= sh, pltpu.roll(x, sh, len(x.shape) - 1), 0.0)
        sh *= 2
    return x


def _fox_prep_kernel(fb_ref, bf_ref, lf_ref, c_ref):
    logf = jax.nn.log_sigmoid(fb_ref[0] + bf_ref[...])
    lf_ref[0] = logf
    c_ref[0] = _cumsum_lanes(logf)


def fox_prep(fbt, b_f):
    B, H, T = fbt.shape
    blk = pl.BlockSpec((1, H, T), lambda b: (b, 0, 0))
    return pl.pallas_call(
        _fox_prep_kernel, grid=(B,),
        in_specs=[blk, pl.BlockSpec((H, 1), lambda b: (0, 0))],
        out_specs=[blk, blk],
        out_shape=[jax.ShapeDtypeStruct((B, H, T), F32)] * 2,
        compiler_params=_cparams(("parallel",)), name="fox_prep",
    )(fbt, b_f.reshape(H, 1))


def _fox_prompt_kernel(c_ref, q_ref, k_ref, v_ref, o_ref, *, tq, s_len, nq):
    i = pl.program_id(1)
    qpos = i * tq + lax.broadcasted_iota(jnp.int32, (tq, 1), 0)

    def body(n_keys):
        kpos = lax.broadcasted_iota(jnp.int32, (1, n_keys), 1)
        causal_bias = jnp.where(kpos <= qpos, 0.0, NEG_BIG)

        def bias(h):
            return causal_bias - c_ref[0, h:h + 1, :n_keys]

        _attend_heads(q_ref, k_ref, v_ref, o_ref, bias, N_HEADS_B, n_keys)

    _causal_span_dispatch(i, nq, s_len, body)


def fox_prompt(c, q, k, v, *, batch, s_len, tq=128):
    nq = s_len // tq
    wb = N_HEADS_B * HEAD_DIM
    kv_spec = pl.BlockSpec((s_len, wb), lambda b, i: (b, 1), pipeline_mode=pl.Buffered(1))
    return pl.pallas_call(
        functools.partial(_fox_prompt_kernel, tq=tq, s_len=s_len, nq=nq),
        grid=(batch, nq),
        in_specs=[pl.BlockSpec((1, N_HEADS_B, s_len), lambda b, i: (b, 0, 0)),
                  pl.BlockSpec((tq, wb), lambda b, i: (b * nq + i, 1)),
                  kv_spec, kv_spec],
        out_specs=pl.BlockSpec((tq, wb), lambda b, i: (b * nq + i, 0)),
        out_shape=jax.ShapeDtypeStruct((batch * s_len, wb), BF16),
        compiler_params=_cparams(("parallel", "arbitrary")), name="fox_prompt",
    )(c, q, k, v)


def _samp_scores_kernel(pt_ref, *refs, n_pages, pps, t_new):
    cache_refs = refs[:pps]
    kin_ref, qi_ref, wsb_ref, out_ref = refs[pps:]
    p = pl.program_id(1)

    def scores(kp):
        s = lax.dot_general(qi_ref[0], kp.astype(BF16), (((1,), (1,)), ((), ())),
                            preferred_element_type=F32)
        r = jnp.maximum(s, 0.0) * wsb_ref[0]
        acc = r[0:t_new]
        for h in range(1, IDX_HEADS):
            acc = acc + r[h * t_new:(h + 1) * t_new]
        return acc

    @pl.when(p < n_pages // pps)
    def _():
        for u in range(pps):
            out_ref[0, :, u * PAGE_SIZE:(u + 1) * PAGE_SIZE] = scores(cache_refs[u][0])

    @pl.when(p == n_pages // pps)
    def _():
        sc = scores(kin_ref[0])
        t = lax.broadcasted_iota(jnp.int32, sc.shape, 0)
        s_new = lax.broadcasted_iota(jnp.int32, sc.shape, 1)
        out_ref[0] = jnp.full(out_ref.shape[1:], -jnp.inf, F32)
        out_ref[0, :, :PAGE_SIZE] = jnp.where(s_new <= t, sc, -jnp.inf)


def samp_scores(page_table, cache_ik, ki_new_pad, qi_s, wsb):
    Bd, n_pages = page_table.shape
    rows = qi_s.shape[1]
    t_new = rows // IDX_HEADS
    pps = math.gcd(n_pages, SCORE_PAGES_PER_STEP)
    n_steps = n_pages // pps

    def page_spec(u):
        return pl.BlockSpec((1, PAGE_SIZE, IDX_DIM),
                            lambda b, p, pt: (pt[b, jnp.minimum(p, n_steps - 1) * pps + u], 0, 0))

    grid_spec = pltpu.PrefetchScalarGridSpec(
        num_scalar_prefetch=1, grid=(Bd, n_steps + 1),
        in_specs=[page_spec(u) for u in range(pps)] + [
            pl.BlockSpec((1, PAGE_SIZE, IDX_DIM), lambda b, p, pt: (b, 0, 0)),
            pl.BlockSpec((1, rows, IDX_DIM), lambda b, p, pt: (b, 0, 0)),
            pl.BlockSpec((1, rows, LANES), lambda b, p, pt: (b, 0, 0))],
        out_specs=pl.BlockSpec((1, t_new, pps * PAGE_SIZE), lambda b, p, pt: (b, 0, p)))
    return pl.pallas_call(
        functools.partial(_samp_scores_kernel, n_pages=n_pages, pps=pps, t_new=t_new),
        grid_spec=grid_spec,
        out_shape=jax.ShapeDtypeStruct((Bd, t_new, (n_pages + pps) * PAGE_SIZE), F32),
        compiler_params=_cparams(("parallel", "arbitrary")), name="samp_scores",
    )(page_table, *([cache_ik] * pps), ki_new_pad, qi_s, wsb)


def _samp_select_kernel(sc_ref, thr_ref, key_ref, *, topk):
    key_ref[...] = _monotone_key(sc_ref[0])
    rows = key_ref.shape[0]
    thr = _kth_largest_key(lambda: key_ref[...], rows, topk)
    thr_ref[0] = jnp.broadcast_to(thr, (rows, LANES))


def samp_select(sc, topk):
    Bd, T, L = sc.shape
    return pl.pallas_call(
        functools.partial(_samp_select_kernel, topk=topk), grid=(Bd,),
        in_specs=[pl.BlockSpec((1, T, L), lambda b: (b, 0, 0))],
        out_specs=pl.BlockSpec((1, T, LANES), lambda b: (b, 0, 0)),
        out_shape=jax.ShapeDtypeStruct((Bd, T, LANES), jnp.int32),
        scratch_shapes=[pltpu.VMEM((T, L), jnp.int32)],
        compiler_params=_cparams(("parallel",)), name="samp_select",
    )(sc)


def _group_head_rows(grp_ref, r):
    flat = grp_ref.reshape(PAGE_SIZE * SUBLANES, HEAD_DIM)
    return flat[pl.ds(r, PAGE_SIZE, stride=SUBLANES), :]


def _samp_attend_kernel(pt_ref, *refs, n_pages, t_new):
    n_heads = N_HEADS_A + N_HEADS_B
    n_hg = n_heads // SUBLANES
    ck_refs, cv_refs = refs[:n_hg], refs[n_hg:2 * n_hg]
    (clf_ref, sc_ref, thr_ref, q_ref, kn_ref, vn_ref, lfn_ref,
     oa_ref, ob_ref, m_ref, l_ref, acc_ref, carry_ref) = refs[2 * n_hg:]
    p = pl.program_id(1)
    n_rows = N_HEADS_A * t_new

    @pl.when(p == 0)
    def _():
        m_ref[...] = jnp.full_like(m_ref, NEG_BIG)
        l_ref[...] = jnp.zeros_like(l_ref)
        acc_ref[...] = jnp.zeros_like(acc_ref)
        carry_ref[...] = jnp.zeros_like(carry_ref)

    def expand_heads(x):
        return jnp.concatenate(
            [jnp.broadcast_to(x[h:h + 1, :], (t_new, x.shape[1])) for h in range(N_HEADS_B)], axis=0)

    def head_rows(src, head):
        if isinstance(src, tuple):
            rows = _group_head_rows(src[head // SUBLANES], head % SUBLANES)
        else:
            rows = src[0, pl.ds(head, PAGE_SIZE, stride=n_heads), :]
        return rows.astype(BF16)

    def update(g, k_ref, v_ref, logits_fn, mask):
        h0 = g * N_HEADS_A
        s = jnp.concatenate(
            [lax.dot_general(q_ref[0, h0 + h].astype(BF16), head_rows(k_ref, h0 + h),
                             (((1,), (1,)), ((), ())), preferred_element_type=F32)
             for h in range(N_HEADS_A)], axis=0) * ATT_SCALE
        s = jnp.where(mask, logits_fn(s), NEG_BIG)
        m_old = m_ref[g]
        m_new = jnp.maximum(m_old, jnp.max(s, axis=-1, keepdims=True))
        pe = jnp.where(mask, jnp.exp(s - m_new), 0.0)
        alpha = jnp.exp(m_old - m_new)
        l_ref[g] = alpha * l_ref[g] + jnp.sum(pe, axis=-1, keepdims=True)
        pv = jnp.concatenate(
            [jnp.dot(pe[h * t_new:(h + 1) * t_new].astype(BF16), head_rows(v_ref, h0 + h),
                     preferred_element_type=F32) for h in range(N_HEADS_A)], axis=0)
        acc_ref[g] = alpha * acc_ref[g] + pv
        m_ref[g] = m_new

    def sel_mask():
        sel = jnp.where(_monotone_key(sc_ref[0]) >= thr_ref[0], 1.0, 0.0)
        return jnp.concatenate([sel] * N_HEADS_A, axis=0) > 0.5

    @pl.when(p == 0)
    def _():
        row_t = lax.broadcasted_iota(jnp.int32, (n_rows, PAGE_SIZE), 0) % t_new
        s_new = lax.broadcasted_iota(jnp.int32, (n_rows, PAGE_SIZE), 1)
        causal = s_new <= row_t
        update(0, kn_ref, vn_ref, lambda s: s, jnp.logical_and(sel_mask(), causal))
        cn = expand_heads(_cumsum_lanes(lfn_ref[0]))
        update(1, kn_ref, vn_ref, lambda s: s - cn, causal)

    @pl.when(p > 0)
    def _():
        update(0, ck_refs, cv_refs, lambda s: s, sel_mask())
        pre = _cumsum_lanes(clf_ref[0])
        tot = pre[:, PAGE_SIZE - 1:PAGE_SIZE]
        r = expand_heads(tot - pre) + carry_ref[...]
        carry_ref[...] += expand_heads(tot)
        update(1, ck_refs, cv_refs, lambda s: s + r, jnp.full((n_rows, PAGE_SIZE), True))

    @pl.when(p == n_pages)
    def _():
        oa_ref[0] = acc_ref[0] / l_ref[0]
        ob_ref[0] = acc_ref[1] / l_ref[1]


def samp_attend(page_table, cache_k, cache_v, cache_lft, sc, thr, q_hm, kn_pad, vn_pad, lfnt_pad):
    Bd, n_pages = page_table.shape
    n_heads, t_new = q_hm.shape[1], q_hm.shape[2]
    n_rows = N_HEADS_A * t_new
    page_rows = kn_pad.shape[1]
    n_hg = n_heads // SUBLANES

    def page(b, p, pt):
        return pt[b, n_pages - jnp.maximum(p, 1)]

    def sc_col(b, p, pt):
        return jnp.where(p == 0, n_pages, n_pages - jnp.maximum(p, 1))

    def group_spec(hg):
        return pl.BlockSpec((1, PAGE_SIZE, SUBLANES, HEAD_DIM), lambda b, p, pt: (page(b, p, pt), 0, hg, 0))

    kv_specs = [group_spec(hg) for hg in range(n_hg)]
    grid_spec = pltpu.PrefetchScalarGridSpec(
        num_scalar_prefetch=1, grid=(Bd, n_pages + 1),
        in_specs=kv_specs + kv_specs + [
                  pl.BlockSpec((1, N_HEADS_B, PAGE_SIZE), lambda b, p, pt: (page(b, p, pt), 0, 0)),
                  pl.BlockSpec((1, t_new, PAGE_SIZE), lambda b, p, pt: (b, 0, sc_col(b, p, pt))),
                  pl.BlockSpec((1, t_new, LANES), lambda b, p, pt: (b, 0, 0)),
                  pl.BlockSpec((1, n_heads, t_new, HEAD_DIM), lambda b, p, pt: (b, 0, 0, 0)),
                  pl.BlockSpec((1, page_rows, HEAD_DIM), lambda b, p, pt: (b, 0, 0)),
                  pl.BlockSpec((1, page_rows, HEAD_DIM), lambda b, p, pt: (b, 0, 0)),
                  pl.BlockSpec((1, N_HEADS_B, PAGE_SIZE), lambda b, p, pt: (b, 0, 0))],
        out_specs=[pl.BlockSpec((1, n_rows, HEAD_DIM), lambda b, p, pt: (b, 0, 0))] * 2,
        scratch_shapes=[pltpu.VMEM((2, n_rows, 1), F32), pltpu.VMEM((2, n_rows, 1), F32),
                        pltpu.VMEM((2, n_rows, HEAD_DIM), F32), pltpu.VMEM((n_rows, 1), F32)])
    return pl.pallas_call(
        functools.partial(_samp_attend_kernel, n_pages=n_pages, t_new=t_new),
        grid_spec=grid_spec,
        out_shape=[jax.ShapeDtypeStruct((Bd, n_rows, HEAD_DIM), F32)] * 2,
        compiler_params=_cparams(("parallel", "arbitrary")), name="samp_attend",
    )(page_table, *([cache_k] * n_hg), *([cache_v] * n_hg), cache_lft, sc, thr, q_hm, kn_pad, vn_pad, lfnt_pad)


def _rope_kernel(inv_ref, cos_ref, sin_ref, *, pos0):
    n = cos_ref.shape[0]
    pos = (pos0 + lax.broadcasted_iota(jnp.int32, (n, LANES), 0)).astype(F32)
    ang = pos * inv_ref[...]
    cos_ref[...] = jnp.cos(ang)
    sin_ref[...] = jnp.sin(ang)


def rope_tables(pos0, n):
    half = DK_C // 2
    inv = (ROPE_BASE ** (-jnp.arange(half, dtype=F32) / half)).reshape(1, half)
    return pl.pallas_call(
        functools.partial(_rope_kernel, pos0=pos0),
        out_shape=[jax.ShapeDtypeStruct((n, half), F32)] * 2, name="rope_tables",
    )(inv)


def _retention_kernel(*refs, chunk, n_chunks, has_s0):
    if has_s0:
        q_ref, k_ref, v_ref, g_ref, cos_ref, sin_ref, lg_ref, gn_ref, s0_ref, o_ref, sout_ref, st_ref = refs
        st_ref[...] = s0_ref[0, 0]
    else:
        q_ref, k_ref, v_ref, g_ref, cos_ref, sin_ref, lg_ref, gn_ref, o_ref, sout_ref, st_ref = refs
        st_ref[...] = jnp.zeros_like(st_ref)
    half = DK_C // 2
    lg = lg_ref[0][:, :1]
    i_col = lax.broadcasted_iota(jnp.int32, (chunk, 1), 0).astype(F32)
    j_row = lax.broadcasted_iota(jnp.int32, (1, chunk), 1).astype(F32)
    diff = i_col - j_row
    d_intra = jnp.where(diff >= 0, jnp.exp(lg * jnp.maximum(diff, 0.0)), 0.0)
    q_dec = jnp.exp(lg * (i_col + 1.0))
    k_dec = jnp.exp(lg * (chunk - 1.0 - i_col))
    c_dec = jnp.exp(lg * chunk)
    gn = gn_ref[0]

    def rot(x, cos, sin):
        x1, x2 = x[:, :half], x[:, half:]
        return jnp.concatenate([x1 * cos - x2 * sin, x1 * sin + x2 * cos], axis=1)

    def chunk_body(c, carry):
        rows = pl.ds(pl.multiple_of(c * chunk, chunk), chunk)
        cos, sin = cos_ref[rows, :], sin_ref[rows, :]
        qr = rot(q_ref[rows, :], cos, sin) * (DK_C ** -0.5)
        kr = rot(k_ref[rows, :], cos, sin)
        qb = qr.astype(BF16)
        vb = v_ref[rows, :].astype(BF16)
        a = lax.dot_general(qb, kr.astype(BF16), (((1,), (1,)), ((), ())),
                            preferred_element_type=F32) * d_intra
        st = st_ref[...]
        o = (jnp.dot(a.astype(BF16), vb, preferred_element_type=F32)
             + jnp.dot(qb, st.astype(BF16), preferred_element_type=F32) * q_dec)
        kd = (kr * k_dec).astype(BF16)
        st_ref[...] = st * c_dec + lax.dot_general(kd, vb, (((0,), (0,)), ((), ())),
                                                   preferred_element_type=F32)
        mu = jnp.mean(o, axis=-1, keepdims=True)
        oc = o - mu
        var = jnp.mean(oc * oc, axis=-1, keepdims=True)
        on = oc * lax.rsqrt(var + LN_EPS) * gn
        g = g_ref[rows, :]
        o_ref[rows, :] = (g * jax.nn.sigmoid(g) * on).astype(o_ref.dtype)
        return carry

    if n_chunks == 1:
        chunk_body(0, 0)
    else:
        lax.fori_loop(0, n_chunks, chunk_body, 0)
    sout_ref[0, 0] = st_ref[...]


def retention(zq, zk, zv, zg, cos, sin, lg_tab, gn, s0, *, batch, t_len, out_dtype):
    chunk = math.gcd(t_len, RET_CHUNK)
    n_chunks = t_len // chunk
    H = N_HEADS_C
    in_specs = [pl.BlockSpec((t_len, DK_C), lambda b, h: (b, h)),
                pl.BlockSpec((t_len, DK_C), lambda b, h: (b, h)),
                pl.BlockSpec((t_len, DV_C), lambda b, h: (b, h)),
                pl.BlockSpec((t_len, DV_C), lambda b, h: (b, h)),
                pl.BlockSpec((t_len, DK_C // 2), lambda b, h: (0, 0)),
                pl.BlockSpec((t_len, DK_C // 2), lambda b, h: (0, 0)),
                pl.BlockSpec((1, 1, LANES), lambda b, h: (h, 0, 0)),
                pl.BlockSpec((1, 1, DV_C), lambda b, h: (h, 0, 0))]
    args = [zq, zk, zv, zg, cos, sin, lg_tab, gn]
    if s0 is not None:
        in_specs.append(pl.BlockSpec((1, 1, DK_C, DV_C), lambda b, h: (b, h, 0, 0)))
        args.append(s0)
    return pl.pallas_call(
        functools.partial(_retention_kernel, chunk=chunk, n_chunks=n_chunks, has_s0=s0 is not None),
        grid=(batch, H), in_specs=in_specs,
        out_specs=[pl.BlockSpec((t_len, DV_C), lambda b, h: (b, h)),
                   pl.BlockSpec((1, 1, DK_C, DV_C), lambda b, h: (b, h, 0, 0))],
        out_shape=[jax.ShapeDtypeStruct((batch * t_len, H * DV_C), out_dtype),
                   jax.ShapeDtypeStruct((batch, H, DK_C, DV_C), F32)],
        scratch_shapes=[pltpu.VMEM((DK_C, DV_C), F32)],
        compiler_params=_cparams(("parallel", "parallel")), name="retention",
    )(*args)


def _ffn(xp_f, xp_b, xs_f, xs_b, w_up, w_down_bf, layer, g, b):
    (h,), (hs,) = matmul(xp_b, xs_b, w_up, [BF16], layer=layer, act="relu2")
    (y,), (ys,) = matmul(h, hs, w_down_bf, [F32], layer=layer, bk=4096)
    return ln_residual(xp_f, y, g, b), ln_residual(xs_f, ys, g, b)


def kernel(x_prompt, x_sample, cache_k, cache_v, cache_idx_k, cache_logf, state_ret, page_table, w_in_att, b_forget, w_out_att, w_in_ret, gn_ret, w_out_ret, ln_mix_g, ln_mix_b, ln_ffn_g, ln_ffn_b, w_ffn_up, w_ffn_down):
    Bp, S, D = x_prompt.shape
    Bd, T, _ = x_sample.shape
    n_pages = page_table.shape[1]
    P = n_pages * PAGE_SIZE
    wa = N_HEADS_A * HEAD_DIM
    wab = wa + N_HEADS_B * HEAD_DIM
    n_heads = wab // HEAD_DIM

    split_att = (wa, wa, wa, IDX_HEADS * IDX_DIM, IDX_DIM, IDX_HEADS, wa, wa, wa, N_HEADS_B)
    off = np.concatenate([[0], np.cumsum(split_att)])
    w0 = w_in_att[0]
    seg = lambda i: w0[:, off[i]:off[i + 1]]
    w_q = jnp.concatenate([seg(0), seg(6)], axis=1).astype(BF16)
    w_k = jnp.concatenate([seg(1), seg(7)], axis=1).astype(BF16)
    w_v = jnp.concatenate([seg(2), seg(8)], axis=1).astype(BF16)
    n_small = IDX_DIM + IDX_HEADS + N_HEADS_B
    w_sm = jnp.concatenate([seg(4), seg(5), seg(9), jnp.zeros((D, 2 * LANES - n_small), F32)], axis=1).astype(BF16)
    w_down_bf = w_ffn_down.astype(BF16)
    w_out1_bf = w_out_ret.astype(BF16)
    hk, hv = N_HEADS_C * DK_C, N_HEADS_C * DV_C

    xp = x_prompt.reshape(Bp * S, D)
    xs = x_sample.reshape(Bd * T, D)
    xp_b, xs_b = xp.astype(BF16), xs.astype(BF16)

    (q,), (qs,) = matmul(xp_b, xs_b, w_q, [BF16])
    (kf, kb), (kfs, _) = matmul(xp_b, xs_b, w_k, [F32, BF16])
    (vf, vb), (vfs, _) = matmul(xp_b, xs_b, w_v, [F32, BF16])
    (qi,), (qis,) = matmul(xp_b, xs_b, w_in_att, [BF16], layer=0, col0=int(off[3]), n_cols=IDX_HEADS * IDX_DIM,
                           head_major=True)
    (small,), (small_s,) = matmul(xp_b, xs_b, w_sm, [F32])

    ki_p = small[:, :IDX_DIM]
    kit = ki_p.reshape(Bp, S, IDX_DIM).transpose(0, 2, 1).astype(BF16)
    oa = dsa_prompt(qi, small, kit, q, kb, vb, batch=Bp, s_len=S)
    fbt = small[:, IDX_DIM + IDX_HEADS:n_small].reshape(Bp, S, N_HEADS_B).transpose(0, 2, 1)
    lft_p, c_p = fox_prep(fbt, b_forget[0])
    ob = fox_prompt(c_p, q, kb, vb, batch=Bp, s_len=S)

    new_k_p = kf.reshape(1, Bp, S, n_heads, HEAD_DIM)
    new_v_p = vf.reshape(1, Bp, S, n_heads, HEAD_DIM)
    new_ik_p = ki_p.reshape(1, Bp, S, IDX_DIM)
    new_lf_p = lft_p.transpose(0, 2, 1).reshape(1, Bp, S, N_HEADS_B)

    ki_s = small_s[:, :IDX_DIM].reshape(Bd, T, IDX_DIM)
    wi_s = small_s[:, IDX_DIM:IDX_DIM + IDX_HEADS].reshape(Bd, T, IDX_HEADS)
    fbt_s = small_s[:, IDX_DIM + IDX_HEADS:n_small].reshape(Bd, T, N_HEADS_B).transpose(0, 2, 1)
    lft_s, _ = fox_prep(jnp.pad(fbt_s, ((0, 0), (0, 0), (0, LANES - T))), b_forget[0])
    lft_s = lft_s[:, :, :T]
    pad_rows = lambda z: jnp.pad(z, ((0, 0), (0, PAGE_SIZE - T), (0, 0)))
    qi_s = qis.reshape(IDX_HEADS, Bd, T, IDX_DIM).transpose(1, 0, 2, 3).reshape(Bd, IDX_HEADS * T, IDX_DIM)
    wsb = jnp.broadcast_to((wi_s * IDX_SCALE).transpose(0, 2, 1).reshape(Bd, IDX_HEADS * T, 1),
                           (Bd, IDX_HEADS * T, LANES))
    sc = samp_scores(page_table, cache_idx_k[0], pad_rows(ki_s), qi_s, wsb)
    thr = samp_select(sc, min(TOPK_MAX, (P + T) // 4))
    q_hm = qs.reshape(Bd, T, n_heads, HEAD_DIM).transpose(0, 2, 1, 3).astype(F32)
    lfnt_pad = jnp.pad(lft_s, ((0, 0), (0, 0), (0, PAGE_SIZE - T)))
    new_rows = lambda z: pad_rows(z.reshape(Bd, T, wab)).reshape(Bd, PAGE_SIZE * n_heads, HEAD_DIM)
    oa_s, ob_s = samp_attend(page_table, cache_k[0], cache_v[0],
                             cache_logf[0].transpose(0, 2, 1), sc, thr, q_hm,
                             new_rows(kfs), new_rows(vfs), lfnt_pad)
    unrow = lambda o: o.reshape(Bd, N_HEADS_A, T, HEAD_DIM).transpose(0, 2, 1, 3).reshape(Bd * T, wa)
    o_s = jnp.concatenate([unrow(oa_s), unrow(ob_s)], axis=1).astype(BF16)

    new_k_s = kfs.reshape(1, Bd, T, n_heads, HEAD_DIM)
    new_v_s = vfs.reshape(1, Bd, T, n_heads, HEAD_DIM)
    new_ik_s = ki_s.reshape(1, Bd, T, IDX_DIM)
    new_lf_s = lft_s.transpose(0, 2, 1).reshape(1, Bd, T, N_HEADS_B)

    (yp,), (ys,) = matmul(jnp.concatenate([oa, ob], axis=1), o_s, w_out_att, [F32], layer=0)
    xp_f, xp_b = ln_residual(xp, yp, ln_mix_g[0], ln_mix_b[0])
    xs_f, xs_b = ln_residual(xs, ys, ln_mix_g[0], ln_mix_b[0])
    (xp_f, xp_b), (xs_f, xs_b) = _ffn(xp_f, xp_b, xs_f, xs_b, w_ffn_up, w_down_bf, 0, ln_ffn_g[0], ln_ffn_b[0])

    lg = jnp.log1p(-jnp.exp2(-5.0 - jnp.arange(N_HEADS_C, dtype=F32)))
    lg_tab = jnp.broadcast_to(lg[:, None, None], (N_HEADS_C, 1, LANES))
    gn = gn_ret[0].reshape(N_HEADS_C, 1, DV_C)
    ret_proj = lambda c0, n, dt, dts: matmul(xp_b, xs_b, w_in_ret, [dt], [dts], layer=0, col0=c0, n_cols=n)
    (zq,), (zqs,) = ret_proj(0, hk, F32, F32)
    (zk,), (zks,) = ret_proj(hk, hk, F32, F32)
    (zv,), (zvs,) = ret_proj(2 * hk, hv, BF16, F32)
    (zg,), (zgs,) = ret_proj(2 * hk + hv, hv, F32, F32)
    cos_p, sin_p = rope_tables(0, S)
    cos_s, sin_s = rope_tables(P, T)
    gated_p, sp = retention(zq, zk, zv, zg, cos_p, sin_p, lg_tab, gn, None, batch=Bp, t_len=S, out_dtype=BF16)
    gated_s, ss = retention(zqs, zks, zvs, zgs, cos_s, sin_s, lg_tab, gn, state_ret[0], batch=Bd, t_len=T,
                            out_dtype=F32)
    (yp,), (ys,) = matmul(gated_p, gated_s.astype(BF16), w_out1_bf, [F32], layer=0, bk=4096)
    xp_f, xp_b = ln_residual(xp_f, yp, ln_mix_g[1], ln_mix_b[1])
    xs_f, xs_b = ln_residual(xs_f, ys, ln_mix_g[1], ln_mix_b[1])
    (yp_out, _), (ys_out, _) = _ffn(xp_f, xp_b, xs_f, xs_b, w_ffn_up, w_down_bf, 1, ln_ffn_g[1], ln_ffn_b[1])

    return (yp_out.reshape(Bp, S, D), ys_out.reshape(Bd, T, D),
            new_k_p, new_v_p, new_ik_p, new_lf_p, sp[None],
            new_k_s, new_v_s, new_ik_s, new_lf_s, ss[None])
```

```python
import functools
import math

import jax
import jax.numpy as jnp
import numpy as np
from jax import lax
from jax.experimental import pallas as pl
from jax.experimental.pallas import tpu as pltpu

F32 = jnp.float32
BF16 = jnp.bfloat16

HEAD_DIM = 128
N_HEADS_A = 16
N_HEADS_B = 16
IDX_HEADS = 64
IDX_DIM = 128
IDX_SCALE = (IDX_HEADS * IDX_DIM) ** -0.5
TOPK_MAX = 256
PAGE_SIZE = 128
DK_C = 256
DV_C = 512
N_HEADS_C = 16
RET_CHUNK = 128
ROPE_BASE = 10000.0
DEPTH = 2
ALPHA = (2 * DEPTH) ** 0.25
LN_EPS = 1e-5
ATT_SCALE = HEAD_DIM ** -0.5

LANES = 128
SUBLANES = 8
VMEM_LIMIT_BYTES = 56 * 1024 * 1024
NEG_BIG = -1e30
INT_MIN = -2 ** 31
CAUSAL_SPANS = 4
MM_BLOCK = 1024
SCORE_PAGES_PER_STEP = 8


def _cparams(sem):
    return pltpu.CompilerParams(dimension_semantics=sem, vmem_limit_bytes=VMEM_LIMIT_BYTES)


def _mm_store(outs, acc, act):
    if act == "relu2":
        r = jnp.maximum(acc, 0.0)
        acc = r * r
    for o in outs:
        if len(o.shape) == 3:
            for hh in range(o.shape[0]):
                o[hh] = acc[:, hh * LANES:(hh + 1) * LANES].astype(o.dtype)
        else:
            o[...] = acc.astype(o.dtype)


def _mm_kernel(x_ref, xs_ref, w_ref, *refs, nk, act, n_out):
    outs, outs_s = refs[:n_out], refs[n_out:2 * n_out]
    scratch = refs[2 * n_out:]
    i = pl.program_id(1)
    if nk == 1:
        _mm_store(outs, jnp.dot(x_ref[...], w_ref[...], preferred_element_type=F32), act)

        @pl.when(i == 0)
        def _():
            _mm_store(outs_s, jnp.dot(xs_ref[...], w_ref[...], preferred_element_type=F32), act)
    else:
        acc_ref, acc_s_ref = scratch
        k = pl.program_id(2)

        @pl.when(k == 0)
        def _():
            acc_ref[...] = jnp.zeros_like(acc_ref)

        acc_ref[...] += jnp.dot(x_ref[...], w_ref[...], preferred_element_type=F32)

        @pl.when(k == nk - 1)
        def _():
            _mm_store(outs, acc_ref[...], act)

        @pl.when(i == 0)
        def _():
            @pl.when(k == 0)
            def _():
                acc_s_ref[...] = jnp.zeros_like(acc_s_ref)

            acc_s_ref[...] += jnp.dot(xs_ref[...], w_ref[...], preferred_element_type=F32)

            @pl.when(k == nk - 1)
            def _():
                _mm_store(outs_s, acc_s_ref[...], act)


def matmul(x, xs, w, out_dtypes, side_dtypes=None, *, layer=None, col0=0, n_cols=None, bk=None, act=None,
           head_major=False):
    M, K = x.shape
    Ms = xs.shape[0]
    assert w.shape[-2] == K and (layer is None) == (len(w.shape) == 2)
    N = n_cols or w.shape[-1]
    side_dtypes = side_dtypes or out_dtypes
    bk = bk or K
    bm = min(MM_BLOCK, M)
    bn = min(MM_BLOCK, N)
    assert M % bm == 0 and N % bn == 0 and K % bk == 0 and col0 % bn == 0
    nk = K // bk
    jb0 = col0 // bn
    if layer is None:
        w_spec = pl.BlockSpec((bk, bn), lambda j, i, k: (k, j + jb0))
    else:
        w_spec = pl.BlockSpec((None, bk, bn), lambda j, i, k: (layer, k, j + jb0))
    in_specs = [pl.BlockSpec((bm, bk), lambda j, i, k: (i, k)),
                pl.BlockSpec((Ms, bk), lambda j, i, k: (0, k)),
                w_spec]
    if head_major:
        hb = bn // LANES
        out_specs = ([pl.BlockSpec((hb, bm, LANES), lambda j, i, k: (j, i, 0)) for _ in out_dtypes]
                     + [pl.BlockSpec((hb, Ms, LANES), lambda j, i, k: (j, 0, 0)) for _ in side_dtypes])
        out_shape = ([jax.ShapeDtypeStruct((N // LANES, M, LANES), d) for d in out_dtypes]
                     + [jax.ShapeDtypeStruct((N // LANES, Ms, LANES), d) for d in side_dtypes])
    else:
        out_specs = ([pl.BlockSpec((bm, bn), lambda j, i, k: (i, j)) for _ in out_dtypes]
                     + [pl.BlockSpec((Ms, bn), lambda j, i, k: (0, j)) for _ in side_dtypes])
        out_shape = ([jax.ShapeDtypeStruct((M, N), d) for d in out_dtypes]
                     + [jax.ShapeDtypeStruct((Ms, N), d) for d in side_dtypes])
    scratch = [pltpu.VMEM((bm, bn), F32), pltpu.VMEM((Ms, bn), F32)] if nk > 1 else []
    res = pl.pallas_call(
        functools.partial(_mm_kernel, nk=nk, act=act, n_out=len(out_dtypes)),
        grid=(N // bn, M // bm, nk), in_specs=in_specs, out_specs=out_specs, out_shape=out_shape,
        scratch_shapes=scratch,
        compiler_params=_cparams(("parallel", "arbitrary", "arbitrary")),
        name="matmul",
    )(x, xs, w)
    return res[:len(out_dtypes)], res[len(out_dtypes):]


def _ln_kernel(x_ref, y_ref, g_ref, b_ref, of_ref, ob_ref):
    z = ALPHA * x_ref[...] + y_ref[...]
    mu = jnp.mean(z, axis=-1, keepdims=True)
    zc = z - mu
    var = jnp.mean(zc * zc, axis=-1, keepdims=True)
    r = zc * lax.rsqrt(var + LN_EPS) * g_ref[...] + b_ref[...]
    of_ref[...] = r
    ob_ref[...] = r.astype(BF16)


def ln_residual(x, y, g, b, *, bm=128):
    M, D = x.shape
    bm = min(bm, M)
    row = pl.BlockSpec((bm, D), lambda i: (i, 0))
    vec = pl.BlockSpec((1, D), lambda i: (0, 0))
    return pl.pallas_call(
        _ln_kernel, grid=(M // bm,), in_specs=[row, row, vec, vec], out_specs=[row, row],
        out_shape=[jax.ShapeDtypeStruct((M, D), F32), jax.ShapeDtypeStruct((M, D), BF16)],
        compiler_params=_cparams(("parallel",)), name="ln_residual",
    )(x, y, g.reshape(1, D), b.reshape(1, D))


def _monotone_key(x):
    bits = lax.bitcast_convert_type(x, jnp.int32)
    return jnp.where(bits < 0, bits ^ jnp.int32(0x7FFFFFFF), bits)


def _kth_largest_key(load_keys, rows, topk):
    def radix_body(it, u):
        cand_u = u | jnp.left_shift(jnp.int32(1), 31 - it)
        cand = cand_u ^ jnp.int32(INT_MIN)
        cnt = jnp.sum(jnp.where(load_keys() >= cand, 1.0, 0.0), axis=-1, keepdims=True)
        return jnp.where(cnt >= float(topk), cand_u, u)

    u = lax.fori_loop(0, 32, radix_body, jnp.zeros((rows, 1), jnp.int32))
    return u ^ jnp.int32(INT_MIN)


def _attend_heads(q_ref, k_ref, v_ref, o_ref, bias_fn, n_heads, n_keys):
    for h in range(n_heads):
        cs = slice(h * HEAD_DIM, (h + 1) * HEAD_DIM)
        s = lax.dot_general(q_ref[:, cs], k_ref[:n_keys, cs], (((1,), (1,)), ((), ())),
                            preferred_element_type=F32) * ATT_SCALE + bias_fn(h)
        m = jnp.max(s, axis=-1, keepdims=True)
        p = jnp.exp(s - m)
        l = jnp.sum(p, axis=-1, keepdims=True)
        o = jnp.dot(p.astype(BF16), v_ref[:n_keys, cs], preferred_element_type=F32) / l
        o_ref[:, cs] = o.astype(o_ref.dtype)


def _causal_span_dispatch(i, nq, s_len, body):
    spans = min(CAUSAL_SPANS, nq)
    per = nq // spans
    for j in range(spans):
        @pl.when(i // per == j)
        def _(j=j):
            body((j + 1) * (s_len // spans))


def _dsa_prompt_kernel(qi_ref, wi_ref, kit_ref, q_ref, k_ref, v_ref, o_ref, sc_ref, key_ref, wb_ref,
                       *, tq, s_len, nq, topk):
    i = pl.program_id(1)
    w = wi_ref[:, :IDX_HEADS] * IDX_SCALE
    for h in range(IDX_HEADS):
        wb_ref[h] = jnp.broadcast_to(w[:, h:h + 1], (tq, LANES))
    qpos = i * tq + lax.broadcasted_iota(jnp.int32, (tq, 1), 0)

    def body(n_keys):
        kpos = lax.broadcasted_iota(jnp.int32, (1, n_keys), 1)
        causal = kpos <= qpos
        sc_ref[:, :n_keys] = jnp.zeros((tq, n_keys), F32)

        def head_body(h, carry):
            s = jnp.dot(qi_ref[h], kit_ref[0, :, :n_keys], preferred_element_type=F32)
            wfull = jnp.concatenate([wb_ref[h]] * (n_keys // LANES), axis=1)
            sc_ref[:, :n_keys] += jnp.maximum(s, 0.0) * wfull
            return carry

        lax.fori_loop(0, IDX_HEADS, head_body, 0, unroll=2)

        key_ref[:, :n_keys] = _monotone_key(jnp.where(causal, sc_ref[:, :n_keys], -jnp.inf))
        thr = _kth_largest_key(lambda: key_ref[:, :n_keys], tq, topk)
        sel = jnp.logical_and(key_ref[:, :n_keys] >= thr, causal)
        sc_ref[:, :n_keys] = jnp.where(sel, 0.0, NEG_BIG)
        _attend_heads(q_ref, k_ref, v_ref, o_ref, lambda h: sc_ref[:, :n_keys], N_HEADS_A, n_keys)

    _causal_span_dispatch(i, nq, s_len, body)


def dsa_prompt(qi_hm, small, kit, q, k, v, *, batch, s_len, tq=128):
    nq = s_len // tq
    wa = N_HEADS_A * HEAD_DIM
    topk = min(TOPK_MAX, s_len // 4)
    kv_spec = pl.BlockSpec((s_len, wa), lambda b, i: (b, 0), pipeline_mode=pl.Buffered(1))
    return pl.pallas_call(
        functools.partial(_dsa_prompt_kernel, tq=tq, s_len=s_len, nq=nq, topk=topk),
        grid=(batch, nq),
        in_specs=[pl.BlockSpec((IDX_HEADS, tq, IDX_DIM), lambda b, i: (0, b * nq + i, 0)),
                  pl.BlockSpec((tq, LANES), lambda b, i: (b * nq + i, 1)),
                  pl.BlockSpec((1, IDX_DIM, s_len), lambda b, i: (b, 0, 0)),
                  pl.BlockSpec((tq, wa), lambda b, i: (b * nq + i, 0)),
                  kv_spec, kv_spec],
        out_specs=pl.BlockSpec((tq, wa), lambda b, i: (b * nq + i, 0)),
        out_shape=jax.ShapeDtypeStruct((batch * s_len, wa), BF16),
        scratch_shapes=[pltpu.VMEM((tq, s_len), F32), pltpu.VMEM((tq, s_len), jnp.int32),
                        pltpu.VMEM((IDX_HEADS, tq, LANES), F32)],
        compiler_params=_cparams(("parallel", "arbitrary")), name="dsa_prompt",
    )(qi_hm, small, kit, q, k, v)


def _cumsum_lanes(x):
    n = x.shape[-1]
    lane = lax.broadcasted_iota(jnp.int32, x.shape, len(x.shape) - 1)
    sh = 1
    while sh < n:
        x = x + jnp.where(lane >= sh, pltpu.roll(x, sh, len(x.shape) - 1), 0.0)
        sh *= 2
    return x


def _fox_prep_kernel(fb_ref, bf_ref, lf_ref, c_ref):
    logf = jax.nn.log_sigmoid(fb_ref[0] + bf_ref[...])
    lf_ref[0] = logf
    c_ref[0] = _cumsum_lanes(logf)


def fox_prep(fbt, b_f):
    B, H, T = fbt.shape
    blk = pl.BlockSpec((1, H, T), lambda b: (b, 0, 0))
    return pl.pallas_call(
        _fox_prep_kernel, grid=(B,),
        in_specs=[blk, pl.BlockSpec((H, 1), lambda b: (0, 0))],
        out_specs=[blk, blk],
        out_shape=[jax.ShapeDtypeStruct((B, H, T), F32)] * 2,
        compiler_params=_cparams(("parallel",)), name="fox_prep",
    )(fbt, b_f.reshape(H, 1))


def _fox_prompt_kernel(c_ref, q_ref, k_ref, v_ref, o_ref, *, tq, s_len, nq):
    i = pl.program_id(1)
    qpos = i * tq + lax.broadcasted_iota(jnp.int32, (tq, 1), 0)

    def body(n_keys):
        kpos = lax.broadcasted_iota(jnp.int32, (1, n_keys), 1)
        causal_bias = jnp.where(kpos <= qpos, 0.0, NEG_BIG)

        def bias(h):
            return causal_bias - c_ref[0, h:h + 1, :n_keys]

        _attend_heads(q_ref, k_ref, v_ref, o_ref, bias, N_HEADS_B, n_keys)

    _causal_span_dispatch(i, nq, s_len, body)


def fox_prompt(c, q, k, v, *, batch, s_len, tq=128):
    nq = s_len // tq
    wb = N_HEADS_B * HEAD_DIM
    kv_spec = pl.BlockSpec((s_len, wb), lambda b, i: (b, 1), pipeline_mode=pl.Buffered(1))
    return pl.pallas_call(
        functools.partial(_fox_prompt_kernel, tq=tq, s_len=s_len, nq=nq),
        grid=(batch, nq),
        in_specs=[pl.BlockSpec((1, N_HEADS_B, s_len), lambda b, i: (b, 0, 0)),
                  pl.BlockSpec((tq, wb), lambda b, i: (b * nq + i, 1)),
                  kv_spec, kv_spec],
        out_specs=pl.BlockSpec((tq, wb), lambda b, i: (b * nq + i, 0)),
        out_shape=jax.ShapeDtypeStruct((batch * s_len, wb), BF16),
        compiler_params=_cparams(("parallel", "arbitrary")), name="fox_prompt",
    )(c, q, k, v)


def _samp_scores_kernel(pt_ref, *refs, n_pages, pps, t_new):
    cache_refs = refs[:pps]
    kin_ref, qi_ref, wsb_ref, out_ref = refs[pps:]
    p = pl.program_id(1)

    def scores(kp):
        s = lax.dot_general(qi_ref[0], kp.astype(BF16), (((1,), (1,)), ((), ())),
                            preferred_element_type=F32)
        r = jnp.maximum(s, 0.0) * wsb_ref[0]
        acc = r[0:t_new]
        for h in range(1, IDX_HEADS):
            acc = acc + r[h * t_new:(h + 1) * t_new]
        return acc

    @pl.when(p < n_pages // pps)
    def _():
        for u in range(pps):
            out_ref[0, :, u * PAGE_SIZE:(u + 1) * PAGE_SIZE] = scores(cache_refs[u][0])

    @pl.when(p == n_pages // pps)
    def _():
        sc = scores(kin_ref[0])
        t = lax.broadcasted_iota(jnp.int32, sc.shape, 0)
        s_new = lax.broadcasted_iota(jnp.int32, sc.shape, 1)
        out_ref[0] = jnp.full(out_ref.shape[1:], -jnp.inf, F32)
        out_ref[0, :, :PAGE_SIZE] = jnp.where(s_new <= t, sc, -jnp.inf)


def samp_scores(page_table, cache_ik, ki_new_pad, qi_s, wsb):
    Bd, n_pages = page_table.shape
    rows = qi_s.shape[1]
    t_new = rows // IDX_HEADS
    pps = math.gcd(n_pages, SCORE_PAGES_PER_STEP)
    n_steps = n_pages // pps

    def page_spec(u):
        return pl.BlockSpec((1, PAGE_SIZE, IDX_DIM),
                            lambda b, p, pt: (pt[b, jnp.minimum(p, n_steps - 1) * pps + u], 0, 0))

    grid_spec = pltpu.PrefetchScalarGridSpec(
        num_scalar_prefetch=1, grid=(Bd, n_steps + 1),
        in_specs=[page_spec(u) for u in range(pps)] + [
            pl.BlockSpec((1, PAGE_SIZE, IDX_DIM), lambda b, p, pt: (b, 0, 0)),
            pl.BlockSpec((1, rows, IDX_DIM), lambda b, p, pt: (b, 0, 0)),
            pl.BlockSpec((1, rows, LANES), lambda b, p, pt: (b, 0, 0))],
        out_specs=pl.BlockSpec((1, t_new, pps * PAGE_SIZE), lambda b, p, pt: (b, 0, p)))
    return pl.pallas_call(
        functools.partial(_samp_scores_kernel, n_pages=n_pages, pps=pps, t_new=t_new),
        grid_spec=grid_spec,
        out_shape=jax.ShapeDtypeStruct((Bd, t_new, (n_pages + pps) * PAGE_SIZE), F32),
        compiler_params=_cparams(("parallel", "arbitrary")), name="samp_scores",
    )(page_table, *([cache_ik] * pps), ki_new_pad, qi_s, wsb)


def _samp_select_kernel(sc_ref, thr_ref, key_ref, *, topk):
    key_ref[...] = _monotone_key(sc_ref[0])
    rows = key_ref.shape[0]
    thr = _kth_largest_key(lambda: key_ref[...], rows, topk)
    thr_ref[0] = jnp.broadcast_to(thr, (rows, LANES))


def samp_select(sc, topk):
    Bd, T, L = sc.shape
    return pl.pallas_call(
        functools.partial(_samp_select_kernel, topk=topk), grid=(Bd,),
        in_specs=[pl.BlockSpec((1, T, L), lambda b: (b, 0, 0))],
        out_specs=pl.BlockSpec((1, T, LANES), lambda b: (b, 0, 0)),
        out_shape=jax.ShapeDtypeStruct((Bd, T, LANES), jnp.int32),
        scratch_shapes=[pltpu.VMEM((T, L), jnp.int32)],
        compiler_params=_cparams(("parallel",)), name="samp_select",
    )(sc)


def _group_head_rows(grp_ref, r):
    flat = grp_ref.reshape(PAGE_SIZE * SUBLANES, HEAD_DIM)
    return flat[pl.ds(r, PAGE_SIZE, stride=SUBLANES), :]


def _samp_attend_kernel(pt_ref, *refs, n_pages, t_new):
    n_heads = N_HEADS_A + N_HEADS_B
    n_hg = n_heads // SUBLANES
    ck_refs, cv_refs = refs[:n_hg], refs[n_hg:2 * n_hg]
    (clf_ref, sc_ref, thr_ref, q_ref, kn_ref, vn_ref, lfn_ref,
     oa_ref, ob_ref, m_ref, l_ref, acc_ref, carry_ref) = refs[2 * n_hg:]
    p = pl.program_id(1)
    n_rows = N_HEADS_A * t_new

    @pl.when(p == 0)
    def _():
        m_ref[...] = jnp.full_like(m_ref, NEG_BIG)
        l_ref[...] = jnp.zeros_like(l_ref)
        acc_ref[...] = jnp.zeros_like(acc_ref)
        carry_ref[...] = jnp.zeros_like(carry_ref)

    def expand_heads(x):
        return jnp.concatenate(
            [jnp.broadcast_to(x[h:h + 1, :], (t_new, x.shape[1])) for h in range(N_HEADS_B)], axis=0)

    def head_rows(src, head):
        if isinstance(src, tuple):
            rows = _group_head_rows(src[head // SUBLANES], head % SUBLANES)
        else:
            rows = src[0, pl.ds(head, PAGE_SIZE, stride=n_heads), :]
        return rows.astype(BF16)

    def update(g, k_ref, v_ref, logits_fn, mask):
        h0 = g * N_HEADS_A
        s = jnp.concatenate(
            [lax.dot_general(q_ref[0, h0 + h].astype(BF16), head_rows(k_ref, h0 + h),
                             (((1,), (1,)), ((), ())), preferred_element_type=F32)
             for h in range(N_HEADS_A)], axis=0) * ATT_SCALE
        s = jnp.where(mask, logits_fn(s), NEG_BIG)
        m_old = m_ref[g]
        m_new = jnp.maximum(m_old, jnp.max(s, axis=-1, keepdims=True))
        pe = jnp.where(mask, jnp.exp(s - m_new), 0.0)
        alpha = jnp.exp(m_old - m_new)
        l_ref[g] = alpha * l_ref[g] + jnp.sum(pe, axis=-1, keepdims=True)
        pv = jnp.concatenate(
            [jnp.dot(pe[h * t_new:(h + 1) * t_new].astype(BF16), head_rows(v_ref, h0 + h),
                     preferred_element_type=F32) for h in range(N_HEADS_A)], axis=0)
        acc_ref[g] = alpha * acc_ref[g] + pv
        m_ref[g] = m_new

    def sel_mask():
        sel = jnp.where(_monotone_key(sc_ref[0]) >= thr_ref[0], 1.0, 0.0)
        return jnp.concatenate([sel] * N_HEADS_A, axis=0) > 0.5

    @pl.when(p == 0)
    def _():
        row_t = lax.broadcasted_iota(jnp.int32, (n_rows, PAGE_SIZE), 0) % t_new
        s_new = lax.broadcasted_iota(jnp.int32, (n_rows, PAGE_SIZE), 1)
        causal = s_new <= row_t
        update(0, kn_ref, vn_ref, lambda s: s, jnp.logical_and(sel_mask(), causal))
        cn = expand_heads(_cumsum_lanes(lfn_ref[0]))
        update(1, kn_ref, vn_ref, lambda s: s - cn, causal)

    @pl.when(p > 0)
    def _():
        update(0, ck_refs, cv_refs, lambda s: s, sel_mask())
        pre = _cumsum_lanes(clf_ref[0])
        tot = pre[:, PAGE_SIZE - 1:PAGE_SIZE]
        r = expand_heads(tot - pre) + carry_ref[...]
        carry_ref[...] += expand_heads(tot)
        update(1, ck_refs, cv_refs, lambda s: s + r, jnp.full((n_rows, PAGE_SIZE), True))

    @pl.when(p == n_pages)
    def _():
        oa_ref[0] = acc_ref[0] / l_ref[0]
        ob_ref[0] = acc_ref[1] / l_ref[1]


def samp_attend(page_table, cache_k, cache_v, cache_lft, sc, thr, q_hm, kn_pad, vn_pad, lfnt_pad):
    Bd, n_pages = page_table.shape
    n_heads, t_new = q_hm.shape[1], q_hm.shape[2]
    n_rows = N_HEADS_A * t_new
    page_rows = kn_pad.shape[1]
    n_hg = n_heads // SUBLANES

    def page(b, p, pt):
        return pt[b, n_pages - jnp.maximum(p, 1)]

    def sc_col(b, p, pt):
        return jnp.where(p == 0, n_pages, n_pages - jnp.maximum(p, 1))

    def group_spec(hg):
        return pl.BlockSpec((1, PAGE_SIZE, SUBLANES, HEAD_DIM), lambda b, p, pt: (page(b, p, pt), 0, hg, 0))

    kv_specs = [group_spec(hg) for hg in range(n_hg)]
    grid_spec = pltpu.PrefetchScalarGridSpec(
        num_scalar_prefetch=1, grid=(Bd, n_pages + 1),
        in_specs=kv_specs + kv_specs + [
                  pl.BlockSpec((1, N_HEADS_B, PAGE_SIZE), lambda b, p, pt: (page(b, p, pt), 0, 0)),
                  pl.BlockSpec((1, t_new, PAGE_SIZE), lambda b, p, pt: (b, 0, sc_col(b, p, pt))),
                  pl.BlockSpec((1, t_new, LANES), lambda b, p, pt: (b, 0, 0)),
                  pl.BlockSpec((1, n_heads, t_new, HEAD_DIM), lambda b, p, pt: (b, 0, 0, 0)),
                  pl.BlockSpec((1, page_rows, HEAD_DIM), lambda b, p, pt: (b, 0, 0)),
                  pl.BlockSpec((1, page_rows, HEAD_DIM), lambda b, p, pt: (b, 0, 0)),
                  pl.BlockSpec((1, N_HEADS_B, PAGE_SIZE), lambda b, p, pt: (b, 0, 0))],
        out_specs=[pl.BlockSpec((1, n_rows, HEAD_DIM), lambda b, p, pt: (b, 0, 0))] * 2,
        scratch_shapes=[pltpu.VMEM((2, n_rows, 1), F32), pltpu.VMEM((2, n_rows, 1), F32),
                        pltpu.VMEM((2, n_rows, HEAD_DIM), F32), pltpu.VMEM((n_rows, 1), F32)])
    return pl.pallas_call(
        functools.partial(_samp_attend_kernel, n_pages=n_pages, t_new=t_new),
        grid_spec=grid_spec,
        out_shape=[jax.ShapeDtypeStruct((Bd, n_rows, HEAD_DIM), F32)] * 2,
        compiler_params=_cparams(("parallel", "arbitrary")), name="samp_attend",
    )(page_table, *([cache_k] * n_hg), *([cache_v] * n_hg), cache_lft, sc, thr, q_hm, kn_pad, vn_pad, lfnt_pad)


def _rope_kernel(inv_ref, cos_ref, sin_ref, *, pos0):
    n = cos_ref.shape[0]
    pos = (pos0 + lax.broadcasted_iota(jnp.int32, (n, LANES), 0)).astype(F32)
    ang = pos * inv_ref[...]
    cos_ref[...] = jnp.cos(ang)
    sin_ref[...] = jnp.sin(ang)


def rope_tables(pos0, n):
    half = DK_C // 2
    inv = (ROPE_BASE ** (-jnp.arange(half, dtype=F32) / half)).reshape(1, half)
    return pl.pallas_call(
        functools.partial(_rope_kernel, pos0=pos0),
        out_shape=[jax.ShapeDtypeStruct((n, half), F32)] * 2, name="rope_tables",
    )(inv)


def _retention_kernel(*refs, chunk, n_chunks, has_s0):
    if has_s0:
        q_ref, k_ref, v_ref, g_ref, cos_ref, sin_ref, lg_ref, gn_ref, s0_ref, o_ref, sout_ref, st_ref = refs
        st_ref[...] = s0_ref[0, 0]
    else:
        q_ref, k_ref, v_ref, g_ref, cos_ref, sin_ref, lg_ref, gn_ref, o_ref, sout_ref, st_ref = refs
        st_ref[...] = jnp.zeros_like(st_ref)
    half = DK_C // 2
    lg = lg_ref[0][:, :1]
    i_col = lax.broadcasted_iota(jnp.int32, (chunk, 1), 0).astype(F32)
    j_row = lax.broadcasted_iota(jnp.int32, (1, chunk), 1).astype(F32)
    diff = i_col - j_row
    d_intra = jnp.where(diff >= 0, jnp.exp(lg * jnp.maximum(diff, 0.0)), 0.0)
    q_dec = jnp.exp(lg * (i_col + 1.0))
    k_dec = jnp.exp(lg * (chunk - 1.0 - i_col))
    c_dec = jnp.exp(lg * chunk)
    gn = gn_ref[0]

    def rot(x, cos, sin):
        x1, x2 = x[:, :half], x[:, half:]
        return jnp.concatenate([x1 * cos - x2 * sin, x1 * sin + x2 * cos], axis=1)

    def chunk_body(c, carry):
        rows = pl.ds(pl.multiple_of(c * chunk, chunk), chunk)
        cos, sin = cos_ref[rows, :], sin_ref[rows, :]
        qr = rot(q_ref[rows, :], cos, sin) * (DK_C ** -0.5)
        kr = rot(k_ref[rows, :], cos, sin)
        qb = qr.astype(BF16)
        vb = v_ref[rows, :].astype(BF16)
        a = lax.dot_general(qb, kr.astype(BF16), (((1,), (1,)), ((), ())),
                            preferred_element_type=F32) * d_intra
        st = st_ref[...]
        o = (jnp.dot(a.astype(BF16), vb, preferred_element_type=F32)
             + jnp.dot(qb, st.astype(BF16), preferred_element_type=F32) * q_dec)
        kd = (kr * k_dec).astype(BF16)
        st_ref[...] = st * c_dec + lax.dot_general(kd, vb, (((0,), (0,)), ((), ())),
                                                   preferred_element_type=F32)
        mu = jnp.mean(o, axis=-1, keepdims=True)
        oc = o - mu
        var = jnp.mean(oc * oc, axis=-1, keepdims=True)
        on = oc * lax.rsqrt(var + LN_EPS) * gn
        g = g_ref[rows, :]
        o_ref[rows, :] = (g * jax.nn.sigmoid(g) * on).astype(o_ref.dtype)
        return carry

    if n_chunks == 1:
        chunk_body(0, 0)
    else:
        lax.fori_loop(0, n_chunks, chunk_body, 0)
    sout_ref[0, 0] = st_ref[...]


def retention(zq, zk, zv, zg, cos, sin, lg_tab, gn, s0, *, batch, t_len, out_dtype):
    chunk = math.gcd(t_len, RET_CHUNK)
    n_chunks = t_len // chunk
    H = N_HEADS_C
    in_specs = [pl.BlockSpec((t_len, DK_C), lambda b, h: (b, h)),
                pl.BlockSpec((t_len, DK_C), lambda b, h: (b, h)),
                pl.BlockSpec((t_len, DV_C), lambda b, h: (b, h)),
                pl.BlockSpec((t_len, DV_C), lambda b, h: (b, h)),
                pl.BlockSpec((t_len, DK_C // 2), lambda b, h: (0, 0)),
                pl.BlockSpec((t_len, DK_C // 2), lambda b, h: (0, 0)),
                pl.BlockSpec((1, 1, LANES), lambda b, h: (h, 0, 0)),
                pl.BlockSpec((1, 1, DV_C), lambda b, h: (h, 0, 0))]
    args = [zq, zk, zv, zg, cos, sin, lg_tab, gn]
    if s0 is not None:
        in_specs.append(pl.BlockSpec((1, 1, DK_C, DV_C), lambda b, h: (b, h, 0, 0)))
        args.append(s0)
    return pl.pallas_call(
        functools.partial(_retention_kernel, chunk=chunk, n_chunks=n_chunks, has_s0=s0 is not None),
        grid=(batch, H), in_specs=in_specs,
        out_specs=[pl.BlockSpec((t_len, DV_C), lambda b, h: (b, h)),
                   pl.BlockSpec((1, 1, DK_C, DV_C), lambda b, h: (b, h, 0, 0))],
        out_shape=[jax.ShapeDtypeStruct((batch * t_len, H * DV_C), out_dtype),
                   jax.ShapeDtypeStruct((batch, H, DK_C, DV_C), F32)],
        scratch_shapes=[pltpu.VMEM((DK_C, DV_C), F32)],
        compiler_params=_cparams(("parallel", "parallel")), name="retention",
    )(*args)


def _ffn(xp_f, xp_b, xs_f, xs_b, w_up, w_down_bf, layer, g, b):
    (h,), (hs,) = matmul(xp_b, xs_b, w_up, [BF16], layer=layer, act="relu2")
    (y,), (ys,) = matmul(h, hs, w_down_bf, [F32], layer=layer, bk=4096)
    return ln_residual(xp_f, y, g, b), ln_residual(xs_f, ys, g, b)


def kernel(x_prompt, x_sample, cache_k, cache_v, cache_idx_k, cache_logf, state_ret, page_table, w_in_att, b_forget, w_out_att, w_in_ret, gn_ret, w_out_ret, ln_mix_g, ln_mix_b, ln_ffn_g, ln_ffn_b, w_ffn_up, w_ffn_down):
    Bp, S, D = x_prompt.shape
    Bd, T, _ = x_sample.shape
    n_pages = page_table.shape[1]
    P = n_pages * PAGE_SIZE
    wa = N_HEADS_A * HEAD_DIM
    wab = wa + N_HEADS_B * HEAD_DIM
    n_heads = wab // HEAD_DIM

    split_att = (wa, wa, wa, IDX_HEADS * IDX_DIM, IDX_DIM, IDX_HEADS, wa, wa, wa, N_HEADS_B)
    off = np.concatenate([[0], np.cumsum(split_att)])
    w0 = w_in_att[0]
    seg = lambda i: w0[:, off[i]:off[i + 1]]
    w_q = jnp.concatenate([seg(0), seg(6)], axis=1).astype(BF16)
    w_k = jnp.concatenate([seg(1), seg(7)], axis=1).astype(BF16)
    w_v = jnp.concatenate([seg(2), seg(8)], axis=1).astype(BF16)
    w_qi = seg(3).astype(BF16)
    n_small = IDX_DIM + IDX_HEADS + N_HEADS_B
    w_sm = jnp.concatenate([seg(4), seg(5), seg(9), jnp.zeros((D, 2 * LANES - n_small), F32)], axis=1).astype(BF16)
    w_out0_bf = w_out_att.astype(BF16)
    w_up_bf = w_ffn_up.astype(BF16)
    w_down_bf = w_ffn_down.astype(BF16)
    w_ret_bf = w_in_ret.astype(BF16)
    w_out1_bf = w_out_ret.astype(BF16)
    hk, hv = N_HEADS_C * DK_C, N_HEADS_C * DV_C

    xp = x_prompt.reshape(Bp * S, D)
    xs = x_sample.reshape(Bd * T, D)
    xp_b, xs_b = xp.astype(BF16), xs.astype(BF16)

    (q,), (qs,) = matmul(xp_b, xs_b, w_q, [BF16])
    (kf, kb), (kfs, _) = matmul(xp_b, xs_b, w_k, [F32, BF16])
    (vf, vb), (vfs, _) = matmul(xp_b, xs_b, w_v, [F32, BF16])
    (qi,), (qis,) = matmul(xp_b, xs_b, w_qi, [BF16], head_major=True)
    (small,), (small_s,) = matmul(xp_b, xs_b, w_sm, [F32])

    ki_p = small[:, :IDX_DIM]
    kit = ki_p.reshape(Bp, S, IDX_DIM).transpose(0, 2, 1).astype(BF16)
    oa = dsa_prompt(qi, small, kit, q, kb, vb, batch=Bp, s_len=S)
    fbt = small[:, IDX_DIM + IDX_HEADS:n_small].reshape(Bp, S, N_HEADS_B).transpose(0, 2, 1)
    lft_p, c_p = fox_prep(fbt, b_forget[0])
    ob = fox_prompt(c_p, q, kb, vb, batch=Bp, s_len=S)

    new_k_p = kf.reshape(1, Bp, S, n_heads, HEAD_DIM)
    new_v_p = vf.reshape(1, Bp, S, n_heads, HEAD_DIM)
    new_ik_p = ki_p.reshape(1, Bp, S, IDX_DIM)
    new_lf_p = lft_p.transpose(0, 2, 1).reshape(1, Bp, S, N_HEADS_B)

    ki_s = small_s[:, :IDX_DIM].reshape(Bd, T, IDX_DIM)
    wi_s = small_s[:, IDX_DIM:IDX_DIM + IDX_HEADS].reshape(Bd, T, IDX_HEADS)
    fbt_s = small_s[:, IDX_DIM + IDX_HEADS:n_small].reshape(Bd, T, N_HEADS_B).transpose(0, 2, 1)
    lft_s, _ = fox_prep(jnp.pad(fbt_s, ((0, 0), (0, 0), (0, LANES - T))), b_forget[0])
    lft_s = lft_s[:, :, :T]
    pad_rows = lambda z: jnp.pad(z, ((0, 0), (0, PAGE_SIZE - T), (0, 0)))
    qi_s = qis.reshape(IDX_HEADS, Bd, T, IDX_DIM).transpose(1, 0, 2, 3).reshape(Bd, IDX_HEADS * T, IDX_DIM)
    wsb = jnp.broadcast_to((wi_s * IDX_SCALE).transpose(0, 2, 1).reshape(Bd, IDX_HEADS * T, 1),
                           (Bd, IDX_HEADS * T, LANES))
    sc = samp_scores(page_table, cache_idx_k[0], pad_rows(ki_s), qi_s, wsb)
    thr = samp_select(sc, min(TOPK_MAX, (P + T) // 4))
    q_hm = qs.reshape(Bd, T, n_heads, HEAD_DIM).transpose(0, 2, 1, 3).astype(F32)
    lfnt_pad = jnp.pad(lft_s, ((0, 0), (0, 0), (0, PAGE_SIZE - T)))
    new_rows = lambda z: pad_rows(z.reshape(Bd, T, wab)).reshape(Bd, PAGE_SIZE * n_heads, HEAD_DIM)
    oa_s, ob_s = samp_attend(page_table, cache_k[0], cache_v[0],
                             cache_logf[0].transpose(0, 2, 1), sc, thr, q_hm,
                             new_rows(kfs), new_rows(vfs), lfnt_pad)
    unrow = lambda o: o.reshape(Bd, N_HEADS_A, T, HEAD_DIM).transpose(0, 2, 1, 3).reshape(Bd * T, wa)
    o_s = jnp.concatenate([unrow(oa_s), unrow(ob_s)], axis=1).astype(BF16)

    new_k_s = kfs.reshape(1, Bd, T, n_heads, HEAD_DIM)
    new_v_s = vfs.reshape(1, Bd, T, n_heads, HEAD_DIM)
    new_ik_s = ki_s.reshape(1, Bd, T, IDX_DIM)
    new_lf_s = lft_s.transpose(0, 2, 1).reshape(1, Bd, T, N_HEADS_B)

    (yp,), (ys,) = matmul(jnp.concatenate([oa, ob], axis=1), o_s, w_out0_bf, [F32], layer=0)
    xp_f, xp_b = ln_residual(xp, yp, ln_mix_g[0], ln_mix_b[0])
    xs_f, xs_b = ln_residual(xs, ys, ln_mix_g[0], ln_mix_b[0])
    (xp_f, xp_b), (xs_f, xs_b) = _ffn(xp_f, xp_b, xs_f, xs_b, w_up_bf, w_down_bf, 0, ln_ffn_g[0], ln_ffn_b[0])

    lg = jnp.log1p(-jnp.exp2(-5.0 - jnp.arange(N_HEADS_C, dtype=F32)))
    lg_tab = jnp.broadcast_to(lg[:, None, None], (N_HEADS_C, 1, LANES))
    gn = gn_ret[0].reshape(N_HEADS_C, 1, DV_C)
    ret_proj = lambda c0, n, dt, dts: matmul(xp_b, xs_b, w_ret_bf, [dt], [dts], layer=0, col0=c0, n_cols=n)
    (zq,), (zqs,) = ret_proj(0, hk, F32, F32)
    (zk,), (zks,) = ret_proj(hk, hk, F32, F32)
    (zv,), (zvs,) = ret_proj(2 * hk, hv, BF16, F32)
    (zg,), (zgs,) = ret_proj(2 * hk + hv, hv, F32, F32)
    cos_p, sin_p = rope_tables(0, S)
    cos_s, sin_s = rope_tables(P, T)
    gated_p, sp = retention(zq, zk, zv, zg, cos_p, sin_p, lg_tab, gn, None, batch=Bp, t_len=S, out_dtype=BF16)
    gated_s, ss = retention(zqs, zks, zvs, zgs, cos_s, sin_s, lg_tab, gn, state_ret[0], batch=Bd, t_len=T,
                            out_dtype=F32)
    (yp,), (ys,) = matmul(gated_p, gated_s.astype(BF16), w_out1_bf, [F32], layer=0, bk=4096)
    xp_f, xp_b = ln_residual(xp_f, yp, ln_mix_g[1], ln_mix_b[1])
    xs_f, xs_b = ln_residual(xs_f, ys, ln_mix_g[1], ln_mix_b[1])
    (yp_out, _), (ys_out, _) = _ffn(xp_f, xp_b, xs_f, xs_b, w_up_bf, w_down_bf, 1, ln_ffn_g[1], ln_ffn_b[1])

    return (yp_out.reshape(Bp, S, D), ys_out.reshape(Bd, T, D),
            new_k_p, new_v_p, new_ik_p, new_lf_p, sp[None],
            new_k_s, new_v_s, new_ik_s, new_lf_s, ss[None])
```
